```python
import math
import jax
import jax.numpy as jnp
from jax import lax
import numpy as np

D_MODEL = 2048
BATCH = 4
SEQ = 4096
DEPTH = 1
DEC_BATCH = 32
DEC_SEQ = 4
PAST_LEN = 16384
PAGE_SIZE = 128

ATT_HEADS = 8
ATT_HD = 128
ATT_KV_HEADS = 2
ATT_GROUP = ATT_HEADS // ATT_KV_HEADS
ATT_WIDTH = ATT_HEADS * ATT_HD
KV_WIDTH = ATT_KV_HEADS * ATT_HD
RW_HD = 64
RW_WIDTH = D_MODEL - ATT_WIDTH
RW_HEADS = RW_WIDTH // RW_HD
MIX_WIDTH = ATT_WIDTH + RW_WIDTH
ROPE_DIM = ATT_HD // 4
ROPE_THETA = 500000.0
CMP_LEN = 32
CMP_STRIDE = 16
CMP_RATIO = CMP_LEN // CMP_STRIDE
CMP_HID = ATT_HD
SLC_BLOCK = 64
SLC_TOPK = 16
WINDOW = 512
Q_BLOCK = 128
DECAY_LORA = 96
A_LORA = 96
GATE_LORA = 256
RW_PROJ = 3 * RW_WIDTH + DECAY_LORA + A_LORA + GATE_LORA
PROJ_SIZES = (ATT_WIDTH,) + (KV_WIDTH,) * 6 + (3 * ATT_HEADS, RW_PROJ)
PROJ_TOTAL = sum(PROJ_SIZES)
PEER_HEADS = 8
PEER_DK = 256
N_KEYS = 128
N_EXPERTS = N_KEYS * N_KEYS
PEER_TOPK = 16
PEER_BLOCK = 128

NORM_EPS = 1e-6
GN_EPS = 64e-5
NEG_INF = -1e30
FORCE = 1e4

kernel_name = 'hymba_nsa_rwkv7_peer_step'


def rmsnorm(x, g):
    xf = x.astype(jnp.float32)
    y = xf * lax.rsqrt(jnp.mean(xf * xf, axis=-1, keepdims=True) + NORM_EPS)
    return (y * g.astype(jnp.float32)).astype(x.dtype)


def rope_partial(x, pos):
    half = ROPE_DIM // 2
    inv = ROPE_THETA ** (-jnp.arange(0, ROPE_DIM, 2, dtype=jnp.float32) / ROPE_DIM)
    ang = pos.astype(jnp.float32)[:, None] * inv[None, :]
    cos = jnp.cos(ang)[None, :, None, :]
    sin = jnp.sin(ang)[None, :, None, :]
    xf = x.astype(jnp.float32)
    x1, x2, rest = xf[..., :half], xf[..., half:ROPE_DIM], xf[..., ROPE_DIM:]
    return jnp.concatenate([x1 * cos - x2 * sin, x1 * sin + x2 * cos, rest], axis=-1).astype(x.dtype)


def masked_softmax(s, mask):
    s = jnp.where(mask, s, NEG_INF)
    e = jnp.where(mask, jnp.exp(s - jnp.max(s, axis=-1, keepdims=True)), 0.0)
    return e / jnp.maximum(jnp.sum(e, axis=-1, keepdims=True), 1e-30)


def project(x, pos, norm1, w_in, q_norm, k_norm_slc, k_norm_win):
    b, t, _ = x.shape
    cols = jnp.split(rmsnorm(x, norm1) @ w_in, [int(o) for o in np.cumsum(PROJ_SIZES)[:-1]], axis=-1)
    q, kc, vc, ks, vs, kw, vw, gates, feat = cols
    kv = lambda z: z.reshape(b, t, ATT_KV_HEADS, ATT_HD)
    q = rope_partial(rmsnorm(q.reshape(b, t, ATT_HEADS, ATT_HD), q_norm), pos)
    kc = rope_partial(kv(kc), pos)
    ks = rope_partial(rmsnorm(kv(ks), k_norm_slc), pos)
    kw = rope_partial(rmsnorm(kv(kw), k_norm_win), pos)
    return q, kc, kv(vc), ks, kv(vs), kw, kv(vw), gates.reshape(b, t, ATT_HEADS, 3), feat


def compress_blocks(x_all, w1, w2, pe):
    b, l, kh, d = x_all.shape
    n_chunk = l // CMP_STRIDE
    n_cmp = n_chunk - CMP_RATIO + 1
    xc = x_all[:, :n_chunk * CMP_STRIDE].reshape(b, n_chunk, CMP_STRIDE, kh, d)
    w1r = w1.reshape(CMP_RATIO, CMP_STRIDE, d, CMP_HID)
    part = jnp.einsum('bnckd,ocdh->bnokh', xc, w1r)
    hid = jnp.einsum('cd,cdh->h', pe, w1)
    for o in range(CMP_RATIO):
        hid = hid + part[:, o:o + n_cmp, o]
    return jnp.einsum('bnkh,hd->bnkd', jax.nn.gelu(hid), w2)


def nsa_attend(q, gates, q_start, k_cmp, v_cmp, k_slc, v_slc, k_win, v_win, win_base,
               cmp_w1_k, cmp_w2_k, cmp_pe_k, cmp_w1_v, cmp_w2_v, cmp_pe_v, k_norm_cmp):
    b, tq = q.shape[0], q.shape[1]
    l = k_cmp.shape[1]
    scale = ATT_HD ** -0.5
    ck = rmsnorm(compress_blocks(k_cmp, cmp_w1_k, cmp_w2_k, cmp_pe_k), k_norm_cmp)
    cv = compress_blocks(v_cmp, cmp_w1_v, cmp_w2_v, cmp_pe_v)
    n_cmp = ck.shape[1]
    cmp_end = jnp.arange(n_cmp) * CMP_STRIDE + CMP_LEN - 1
    n_slc = -(-l // SLC_BLOCK)
    pad = n_slc * SLC_BLOCK - l

    def to_blocks(z):
        z = jnp.pad(z, ((0, 0), (0, pad), (0, 0), (0, 0)))
        return z.reshape(b, n_slc, SLC_BLOCK, ATT_KV_HEADS, ATT_HD).transpose(0, 3, 1, 2, 4)

    ks_blk, vs_blk = to_blocks(k_slc), to_blocks(v_slc)
    ci = jnp.arange(n_cmp)[:, None]
    sj = jnp.arange(n_slc)[None, :]
    overlap = ((ci * CMP_STRIDE < (sj + 1) * SLC_BLOCK)
               & (ci * CMP_STRIDE + CMP_LEN > sj * SLC_BLOCK)).astype(jnp.float32)
    blk_ids = jnp.arange(n_slc)
    top_n = min(SLC_TOPK, n_slc)
    qb = math.gcd(tq, Q_BLOCK)
    n_qb = tq // qb
    qg = q.reshape(b, n_qb, qb, ATT_KV_HEADS, ATT_GROUP, ATT_HD).transpose(1, 0, 2, 3, 4, 5)
    gg = jax.nn.sigmoid(gates.astype(jnp.float32)).reshape(
        b, n_qb, qb, ATT_KV_HEADS, ATT_GROUP, 3).transpose(1, 0, 2, 3, 4, 5)
    bi = jnp.arange(b)[:, None, None, None]
    hi = jnp.arange(ATT_KV_HEADS)[None, :, None, None]
    wofs = jnp.arange(WINDOW + qb)

    def one_block(args):
        qi, gi, idx = args
        s = q_start + idx * qb
        qpos = s + jnp.arange(qb)
        sc = jnp.einsum('bqkgd,bnkd->bkgqn', qi, ck).astype(jnp.float32) * scale
        p_cmp = masked_softmax(sc, cmp_end[None, :] <= qpos[:, None])
        o_cmp = jnp.einsum('bkgqn,bnkd->bqkgd', p_cmp.astype(cv.dtype), cv)
        imp = jnp.einsum('bkgqn,nm->bkqm', p_cmp, overlap)
        cur = (qpos // SLC_BLOCK)[:, None]
        forced = (blk_ids == 0) | (blk_ids == cur) | (blk_ids == cur - 1)
        imp = jnp.where(forced, FORCE, jnp.where(blk_ids <= cur, imp, -FORCE))
        _, sel = lax.top_k(imp, top_n)
        kg = ks_blk[bi, hi, sel].reshape(b, ATT_KV_HEADS, qb, top_n * SLC_BLOCK, ATT_HD)
        vg = vs_blk[bi, hi, sel].reshape(b, ATT_KV_HEADS, qb, top_n * SLC_BLOCK, ATT_HD)
        kpos = (sel[..., None] * SLC_BLOCK + jnp.arange(SLC_BLOCK)).reshape(b, ATT_KV_HEADS, qb, top_n * SLC_BLOCK)
        ss = jnp.einsum('bqkgd,bkqnd->bkgqn', qi, kg).astype(jnp.float32) * scale
        p_slc = masked_softmax(ss, kpos[:, :, None] <= qpos[:, None])
        o_slc = jnp.einsum('bkgqn,bkqnd->bqkgd', p_slc.astype(vg.dtype), vg)
        kw = lax.dynamic_slice_in_dim(k_win, s - WINDOW - win_base, WINDOW + qb, axis=1)
        vw = lax.dynamic_slice_in_dim(v_win, s - WINDOW - win_base, WINDOW + qb, axis=1)
        wpos = (s - WINDOW + wofs)[None, :]
        wmask = (wpos <= qpos[:, None]) & (wpos > qpos[:, None] - WINDOW) & (wpos >= 0)
        sw = jnp.einsum('bqkgd,bwkd->bkgqw', qi, kw).astype(jnp.float32) * scale
        p_win = masked_softmax(sw, wmask)
        o_win = jnp.einsum('bkgqw,bwkd->bqkgd', p_win.astype(vw.dtype), vw)
        o = gi[..., 0:1] * o_cmp + gi[..., 1:2] * o_slc + gi[..., 2:3] * o_win
        return o.astype(q.dtype)

    out = lax.map(one_block, (qg, gg, jnp.arange(n_qb)))
    return out.transpose(1, 0, 2, 3, 4, 5).reshape(b, tq, ATT_HEADS, ATT_HD)


def rwkv_time_mix(feat, prev_feat, wkv0, rw_mu, rw_w0, rw_w2, rw_a0, rw_a2, rw_g2,
                  rw_k_k, rw_k_a, rw_r_k, rw_ln_w, rw_ln_b):
    b, t, _ = feat.shape
    shifted = jnp.concatenate([prev_feat[:, None], feat[:, :-1]], axis=1)
    mixed = feat + (shifted - feat) * rw_mu
    o1 = 3 * RW_WIDTH
    r, k, v, wd, ad, gd = jnp.split(mixed, [RW_WIDTH, 2 * RW_WIDTH, o1, o1 + DECAY_LORA, o1 + DECAY_LORA + A_LORA], axis=-1)
    wlog = -jax.nn.softplus(-(rw_w0 + jnp.tanh(wd) @ rw_w2)) - 0.5
    decay = jnp.exp(-jnp.exp(wlog.astype(jnp.float32)))
    a = jax.nn.sigmoid(rw_a0 + ad @ rw_a2)
    g = jax.nn.sigmoid(gd) @ rw_g2
    hs = lambda z: z.reshape(b, t, RW_HEADS, RW_HD).astype(jnp.float32)
    kk = hs(k * rw_k_k)
    kk = kk / jnp.maximum(jnp.sqrt(jnp.sum(kk * kk, axis=-1, keepdims=True)), 1e-12)
    k = k * (1 + (a - 1) * rw_k_a)
    rh, kh, vh, ah, dh = hs(r), hs(k), hs(v), hs(a), hs(decay)

    def step(S, inp):
        r_t, d_t, k_t, v_t, kk_t, a_t = inp
        sa = jnp.einsum('bhvk,bhk->bhv', S, -kk_t)
        S = S * d_t[:, :, None, :] + sa[..., None] * (kk_t * a_t)[:, :, None, :] + v_t[..., None] * k_t[:, :, None, :]
        return S, jnp.einsum('bhvk,bhk->bhv', S, r_t)

    tm = lambda z: jnp.swapaxes(z, 0, 1)
    s_fin, out = lax.scan(step, wkv0.astype(jnp.float32), (tm(rh), tm(dh), tm(kh), tm(vh), tm(kk), tm(ah)))
    out = tm(out)
    mu = jnp.mean(out, axis=-1, keepdims=True)
    var = jnp.mean(jnp.square(out - mu), axis=-1, keepdims=True)
    out = ((out - mu) * lax.rsqrt(var + GN_EPS)).reshape(b, t, RW_WIDTH) * rw_ln_w + rw_ln_b
    bonus = (jnp.sum(rh * kh * rw_r_k, axis=-1, keepdims=True) * vh).reshape(b, t, RW_WIDTH)
    out = (out + bonus) * g
    return out.astype(feat.dtype), s_fin.astype(wkv0.dtype), feat[:, -1]


def peer_ffn(h, peer_wq, peer_q_norm, peer_subkeys, peer_u, peer_v):
    b, t, d = h.shape
    n = b * t
    n_blk = -(-n // PEER_BLOCK)
    ht = jnp.pad(h.reshape(n, d), ((0, n_blk * PEER_BLOCK - n), (0, 0))).reshape(n_blk, PEER_BLOCK, d)
    half = PEER_DK // 2

    def one_block(hb):
        q = rmsnorm((hb @ peer_wq).reshape(PEER_BLOCK, PEER_HEADS, PEER_DK), peer_q_norm)
        s1 = jnp.einsum('thd,nd->thn', q[..., :half], peer_subkeys[0]).astype(jnp.float32)
        s2 = jnp.einsum('thd,nd->thn', q[..., half:], peer_subkeys[1]).astype(jnp.float32)
        v1, i1 = lax.top_k(s1, PEER_TOPK)
        v2, i2 = lax.top_k(s2, PEER_TOPK)
        cand = (v1[..., :, None] + v2[..., None, :]).reshape(PEER_BLOCK, PEER_HEADS, PEER_TOPK * PEER_TOPK)
        best, c = lax.top_k(cand, PEER_TOPK)
        e = (jnp.take_along_axis(i1, c // PEER_TOPK, axis=-1) * N_KEYS
             + jnp.take_along_axis(i2, c % PEER_TOPK, axis=-1))
        gate = jax.nn.softmax(best, axis=-1)
        act = jax.nn.gelu(jnp.einsum('td,thed->the', hb, peer_u[e]).astype(jnp.float32))
        return jnp.einsum('the,thed->td', (gate * act).astype(hb.dtype), peer_v[e])

    out = lax.map(one_block, ht)
    return out.reshape(n_blk * PEER_BLOCK, d)[:n].reshape(b, t, d)


def combine(x, o_att, o_rw, attn_out_norm, w_out, norm2, peer_wq, peer_q_norm, peer_subkeys, peer_u, peer_v):
    b, t, _ = x.shape
    o_att = rmsnorm(o_att, attn_out_norm.reshape(ATT_HEADS, ATT_HD)).reshape(b, t, ATT_WIDTH)
    x = x + jnp.concatenate([o_att, o_rw], axis=-1) @ w_out
    return x + peer_ffn(rmsnorm(x, norm2), peer_wq, peer_q_norm, peer_subkeys, peer_u, peer_v)


def decoder_layer(xp, xs, c_k_cmp, c_v_cmp, c_k_slc, c_v_slc, s_k_win, s_v_win, s_wkv, s_shift, page_table,
                  norm1, w_in, q_norm, k_norm_slc, k_norm_win, k_norm_cmp,
                  cmp_w1_k, cmp_w2_k, cmp_pe_k, cmp_w1_v, cmp_w2_v, cmp_pe_v, attn_out_norm,
                  rw_mu, rw_w0, rw_w2, rw_a0, rw_a2, rw_g2, rw_k_k, rw_k_a, rw_r_k, rw_ln_w, rw_ln_b,
                  w_out, norm2, peer_wq, peer_q_norm, peer_subkeys, peer_u, peer_v):
    proj_w = (norm1, w_in, q_norm, k_norm_slc, k_norm_win)
    cmp_w = (cmp_w1_k, cmp_w2_k, cmp_pe_k, cmp_w1_v, cmp_w2_v, cmp_pe_v, k_norm_cmp)
    rw_w = (rw_mu, rw_w0, rw_w2, rw_a0, rw_a2, rw_g2, rw_k_k, rw_k_a, rw_r_k, rw_ln_w, rw_ln_b)
    out_w = (attn_out_norm, w_out, norm2, peer_wq, peer_q_norm, peer_subkeys, peer_u, peer_v)

    b, t, _ = xp.shape
    q, kc, vc, ks, vs, kw, vw, gates, feat = project(xp, jnp.arange(t), *proj_w)
    padw = lambda z: jnp.pad(z, ((0, 0), (WINDOW, 0), (0, 0), (0, 0)))
    o_att = nsa_attend(q, gates, 0, kc, vc, ks, vs, padw(kw), padw(vw), -WINDOW, *cmp_w)
    o_rw, wkv_p, shift_p = rwkv_time_mix(feat, jnp.zeros((b, RW_PROJ), feat.dtype),
                                         jnp.zeros((b, RW_HEADS, RW_HD, RW_HD), feat.dtype), *rw_w)
    yp = combine(xp, o_att, o_rw, *out_w)
    wp = min(WINDOW, t)
    p_state = (kc, vc, ks, vs, kw[:, t - wp:], vw[:, t - wp:], wkv_p, shift_p)

    db, ts, _ = xs.shape
    n_past = page_table.shape[1] * PAGE_SIZE
    q, kc, vc, ks, vs, kw, vw, gates, feat = project(xs, n_past + jnp.arange(ts), *proj_w)

    def with_past(cache, new):
        past = cache[page_table].reshape(db, n_past, ATT_KV_HEADS, ATT_HD)
        return jnp.concatenate([past, new], axis=1)

    wbuf = s_k_win.shape[1]

    def win_ext(buf, new):
        return jnp.pad(jnp.concatenate([buf, new], axis=1), ((0, 0), (WINDOW - wbuf, 0), (0, 0), (0, 0)))

    o_att = nsa_attend(q, gates, n_past, with_past(c_k_cmp, kc), with_past(c_v_cmp, vc),
                       with_past(c_k_slc, ks), with_past(c_v_slc, vs),
                       win_ext(s_k_win, kw), win_ext(s_v_win, vw), n_past - WINDOW, *cmp_w)
    o_rw, wkv_s, shift_s = rwkv_time_mix(feat, s_shift, s_wkv, *rw_w)
    ys = combine(xs, o_att, o_rw, *out_w)
    kw_buf = jnp.concatenate([s_k_win, kw], axis=1)[:, -wbuf:]
    vw_buf = jnp.concatenate([s_v_win, vw], axis=1)[:, -wbuf:]
    s_state = (kc, vc, ks, vs, kw_buf, vw_buf, wkv_s, shift_s)
    return yp, ys, p_state, s_state


def setup_inputs(seed: int = 0) -> dict:
    key = jax.random.key(seed)
    keys = iter(jax.random.split(key, 64))
    f32 = jnp.float32
    rnd = lambda shape, scale: jax.random.normal(next(keys), shape, f32) * scale
    par = lambda shape, scale: rnd((DEPTH,) + shape, scale)
    gain = lambda shape: 1.0 + par(shape, 0.01)
    n_pages = PAST_LEN // PAGE_SIZE
    n_used = DEC_BATCH * n_pages
    n_pool = n_used + max(1, n_used // 4)
    pool = (DEPTH, n_pool, PAGE_SIZE, ATT_KV_HEADS, ATT_HD)
    wbuf = min(WINDOW, PAST_LEN)
    return {
        'x_prompt': rnd((BATCH, SEQ, D_MODEL), 1.0),
        'x_sample': rnd((DEC_BATCH, DEC_SEQ, D_MODEL), 1.0),
        'cache_k_cmp': rnd(pool, 1.0),
        'cache_v_cmp': rnd(pool, 1.0),
        'cache_k_slc': rnd(pool, 1.0),
        'cache_v_slc': rnd(pool, 1.0),
        'state_k_win': rnd((DEPTH, DEC_BATCH, wbuf, ATT_KV_HEADS, ATT_HD), 1.0),
        'state_v_win': rnd((DEPTH, DEC_BATCH, wbuf, ATT_KV_HEADS, ATT_HD), 1.0),
        'state_wkv': rnd((DEPTH, DEC_BATCH, RW_HEADS, RW_HD, RW_HD), 0.3),
        'state_shift': rnd((DEPTH, DEC_BATCH, RW_PROJ), 1.0),
        'page_table': jax.random.permutation(next(keys), n_pool)[:n_used].reshape(DEC_BATCH, n_pages).astype(jnp.int32),
        'norm1': gain((D_MODEL,)),
        'w_in': par((D_MODEL, PROJ_TOTAL), D_MODEL ** -0.5),
        'q_norm': gain((ATT_HD,)),
        'k_norm_slc': gain((ATT_HD,)),
        'k_norm_win': gain((ATT_HD,)),
        'k_norm_cmp': gain((ATT_HD,)),
        'cmp_w1_k': par((CMP_LEN, ATT_HD, CMP_HID), (CMP_LEN * ATT_HD) ** -0.5),
        'cmp_w2_k': par((CMP_HID, ATT_HD), CMP_HID ** -0.5),
        'cmp_pe_k': par((CMP_LEN, ATT_HD), 0.1),
        'cmp_w1_v': par((CMP_LEN, ATT_HD, CMP_HID), (CMP_LEN * ATT_HD) ** -0.5),
        'cmp_w2_v': par((CMP_HID, ATT_HD), CMP_HID ** -0.5),
        'cmp_pe_v': par((CMP_LEN, ATT_HD), 0.1),
        'attn_out_norm': gain((ATT_WIDTH,)),
        'rw_mu': jax.random.uniform(next(keys), (DEPTH, RW_PROJ), f32),
        'rw_w0': par((RW_WIDTH,), 0.5),
        'rw_w2': par((DECAY_LORA, RW_WIDTH), DECAY_LORA ** -0.5),
        'rw_a0': par((RW_WIDTH,), 0.5),
        'rw_a2': par((A_LORA, RW_WIDTH), A_LORA ** -0.5),
        'rw_g2': par((GATE_LORA, RW_WIDTH), GATE_LORA ** -0.5),
        'rw_k_k': 0.85 + par((RW_WIDTH,), 0.05),
        'rw_k_a': 1.0 + par((RW_WIDTH,), 0.05),
        'rw_r_k': par((RW_HEADS, RW_HD), 0.1),
        'rw_ln_w': gain((RW_WIDTH,)),
        'rw_ln_b': par((RW_WIDTH,), 0.01),
        'w_out': par((MIX_WIDTH, D_MODEL), MIX_WIDTH ** -0.5),
        'norm2': gain((D_MODEL,)),
        'peer_wq': par((D_MODEL, PEER_HEADS * PEER_DK), D_MODEL ** -0.5),
        'peer_q_norm': gain((PEER_DK,)),
        'peer_subkeys': par((2, N_KEYS, PEER_DK // 2), (PEER_DK // 2) ** -0.5),
        'peer_u': par((N_EXPERTS, D_MODEL), D_MODEL ** -0.5),
        'peer_v': par((N_EXPERTS, D_MODEL), 0.25),
    }


def reference(x_prompt, x_sample, cache_k_cmp, cache_v_cmp, cache_k_slc, cache_v_slc,
              state_k_win, state_v_win, state_wkv, state_shift, page_table,
              norm1, w_in, q_norm, k_norm_slc, k_norm_win, k_norm_cmp,
              cmp_w1_k, cmp_w2_k, cmp_pe_k, cmp_w1_v, cmp_w2_v, cmp_pe_v, attn_out_norm,
              rw_mu, rw_w0, rw_w2, rw_a0, rw_a2, rw_g2, rw_k_k, rw_k_a, rw_r_k, rw_ln_w, rw_ln_b,
              w_out, norm2, peer_wq, peer_q_norm, peer_subkeys, peer_u, peer_v):
    y_prompt, y_sample = x_prompt, x_sample
    p_states, s_states = [], []
    for l in range(DEPTH):
        y_prompt, y_sample, p_st, s_st = decoder_layer(
            y_prompt, y_sample, cache_k_cmp[l], cache_v_cmp[l], cache_k_slc[l], cache_v_slc[l],
            state_k_win[l], state_v_win[l], state_wkv[l], state_shift[l], page_table,
            norm1[l], w_in[l], q_norm[l], k_norm_slc[l], k_norm_win[l], k_norm_cmp[l],
            cmp_w1_k[l], cmp_w2_k[l], cmp_pe_k[l], cmp_w1_v[l], cmp_w2_v[l], cmp_pe_v[l], attn_out_norm[l],
            rw_mu[l], rw_w0[l], rw_w2[l], rw_a0[l], rw_a2[l], rw_g2[l], rw_k_k[l], rw_k_a[l], rw_r_k[l],
            rw_ln_w[l], rw_ln_b[l], w_out[l], norm2[l], peer_wq[l], peer_q_norm[l], peer_subkeys[l],
            peer_u[l], peer_v[l])
        p_states.append(p_st)
        s_states.append(s_st)
    new_state = [jnp.stack(z) for z in zip(*p_states)] + [jnp.stack(z) for z in zip(*s_states)]
    return (y_prompt, y_sample, *new_state)
```

```python
import functools
import math

import jax
import jax.numpy as jnp
from jax import lax
from jax.experimental import pallas as pl
from jax.experimental.pallas import tpu as pltpu

F32, BF16, I32 = jnp.float32, jnp.bfloat16, jnp.int32

D_MODEL = 2048
PAGE = 128
ATT_HEADS, ATT_HD, KV_HEADS, ATT_GROUP = 8, 128, 2, 4
ATT_WIDTH, KV_WIDTH = ATT_HEADS * ATT_HD, KV_HEADS * ATT_HD
RW_HD, RW_WIDTH, RW_HEADS = 64, 1024, 16
ROPE_DIM, ROPE_THETA = 32, 500000.0
CMP_LEN, CMP_STRIDE = 32, 16
SLC_BLOCK, SLC_TOPK, WINDOW, Q_BLOCK = 64, 16, 512, 128
DECAY_LORA, A_LORA, GATE_LORA = 96, 96, 256
LORA_PAD = 128
RW_PROJ = 3 * RW_WIDTH + DECAY_LORA + A_LORA + GATE_LORA
RW_PROJ_PAD = 3 * RW_WIDTH + 2 * LORA_PAD + GATE_LORA
QKV_WIDTH = ATT_WIDTH + 6 * KV_WIDTH
QKVG_PAD = QKV_WIDTH + 128
PEER_HEADS, PEER_DK, PEER_TOPK = 8, 256, 16
NORM_EPS, GN_EPS, NEG_INF, FORCE = 1e-6, 64e-5, -1e30, 1e4
REMOVED = -3e38
VMEM_LIMIT = 56 * 1024 * 1024


def _cp(*sem):
    return pltpu.CompilerParams(dimension_semantics=sem, vmem_limit_bytes=VMEM_LIMIT)


def _dot(a, b):
    return jnp.dot(a.astype(BF16), b.astype(BF16), preferred_element_type=F32)


def _dot_nt(a, b):
    return lax.dot_general(a.astype(BF16), b.astype(BF16), (((1,), (1,)), ((), ())),
                           preferred_element_type=F32)


def _split_dot(x, m_bf16, terms):
    out = None
    for _ in range(terms):
        hi = x.astype(BF16)
        part = jnp.dot(hi, m_bf16, preferred_element_type=F32)
        out = part if out is None else out + part
        x = x - hi.astype(F32)
    return out


def _rms(x, g):
    return x * lax.rsqrt(jnp.mean(x * x, axis=-1, keepdims=True) + NORM_EPS) * g


def _masked_softmax(s, mask):
    s = jnp.where(mask, s, NEG_INF)
    e = jnp.where(mask, jnp.exp(s - jnp.max(s, axis=-1, keepdims=True)), 0.0)
    return e / jnp.maximum(jnp.sum(e, axis=-1, keepdims=True), 1e-30)


def _iota(shape, dim):
    return lax.broadcasted_iota(I32, shape, dim)


def _norm_mm_kernel(x_ref, g_ref, w_ref, o_ref, xn_ref):
    @pl.when(pl.program_id(1) == 0)
    def _():
        xn_ref[...] = _rms(x_ref[...], g_ref[...]).astype(BF16)

    o_ref[...] = jnp.dot(xn_ref[...], w_ref[...], preferred_element_type=F32)


def _norm_matmul(x, g, w_bf16, tn):
    m, k = x.shape
    n = w_bf16.shape[1]
    tm = min(512, m)
    return pl.pallas_call(
        _norm_mm_kernel,
        grid=(m // tm, n // tn),
        in_specs=[pl.BlockSpec((tm, k), lambda i, j: (i, 0)),
                  pl.BlockSpec((1, k), lambda i, j: (0, 0)),
                  pl.BlockSpec((k, tn), lambda i, j: (0, j))],
        out_specs=pl.BlockSpec((tm, tn), lambda i, j: (i, j)),
        out_shape=jax.ShapeDtypeStruct((m, n), F32),
        scratch_shapes=[pltpu.VMEM((tm, k), BF16)],
        compiler_params=_cp("arbitrary", "arbitrary"),
    )(x, g.reshape(1, k), w_bf16)


def _qkv_post_kernel(p_ref, cos_ref, sa_ref, sb_ref, gq_ref, gs_ref, gw_ref,
                     q_ref, kc_ref, vc_ref, ks_ref, vs_ref, kw_ref, vw_ref):
    cos, sa, sb = cos_ref[...], sa_ref[...], sb_ref[...]

    def rope(x):
        return x * cos + pltpu.roll(x, ATT_HD - ROPE_DIM // 2, 1) * sa + pltpu.roll(x, ROPE_DIM // 2, 1) * sb

    def head(c):
        return p_ref[:, c * ATT_HD:(c + 1) * ATT_HD]

    for h in range(ATT_HEADS):
        q_ref[:, h * ATT_HD:(h + 1) * ATT_HD] = rope(_rms(head(h), gq_ref[...]))
    for h in range(KV_HEADS):
        sl = slice(h * ATT_HD, (h + 1) * ATT_HD)
        kc_ref[:, sl] = rope(head(8 + h))
        vc_ref[:, sl] = head(10 + h)
        ks_ref[:, sl] = rope(_rms(head(12 + h), gs_ref[...]))
        vs_ref[:, sl] = head(14 + h)
        kw_ref[:, sl] = rope(_rms(head(16 + h), gw_ref[...]))
        vw_ref[:, sl] = head(18 + h)


def _rope_tables(pos):
    half = ROPE_DIM // 2
    inv = ROPE_THETA ** (-jnp.arange(0, ROPE_DIM, 2, dtype=F32) / ROPE_DIM)
    ang = pos.astype(F32)[:, None] * inv[None, :]
    cos, sin = jnp.cos(ang), jnp.sin(ang)
    n = pos.shape[0]
    rest = ATT_HD - ROPE_DIM
    cos_t = jnp.concatenate([cos, cos, jnp.ones((n, rest), F32)], axis=1)
    sin_a = jnp.concatenate([-sin, jnp.zeros((n, half + rest), F32)], axis=1)
    sin_b = jnp.concatenate([jnp.zeros((n, half), F32), sin, jnp.zeros((n, rest), F32)], axis=1)
    return cos_t, sin_a, sin_b


def _qkv_post(proj, pos, q_norm, k_norm_slc, k_norm_win):
    m = proj.shape[0]
    p = pos.shape[0]
    tm = min(256, m, p)
    reps = p // tm
    tabs = _rope_tables(pos)
    row = lambda w: pl.BlockSpec((tm, w), lambda i: (i, 0))
    tab = pl.BlockSpec((tm, ATT_HD), lambda i: (i % reps, 0))
    gain = pl.BlockSpec((1, ATT_HD), lambda i: (0, 0))
    kv = jax.ShapeDtypeStruct((m, KV_WIDTH), F32)
    return pl.pallas_call(
        _qkv_post_kernel,
        grid=(m // tm,),
        in_specs=[row(QKV_WIDTH), tab, tab, tab, gain, gain, gain],
        out_specs=[row(ATT_WIDTH)] + [row(KV_WIDTH)] * 6,
        out_shape=[jax.ShapeDtypeStruct((m, ATT_WIDTH), F32)] + [kv] * 6,
        compiler_params=_cp("arbitrary"),
    )(proj, *tabs, q_norm.reshape(1, -1), k_norm_slc.reshape(1, -1), k_norm_win.reshape(1, -1))


def _cmp_proj_kernel(pt_ref, *refs, n_in):
    x_refs, w_ref, o_ref, xc_ref = refs[:n_in], refs[n_in], refs[n_in + 1], refs[n_in + 2]
    cpp = PAGE // CMP_STRIDE
    for g in range(n_in):
        for c in range(CMP_STRIDE):
            xc_ref[g * cpp:(g + 1) * cpp, c * ATT_HD:(c + 1) * ATT_HD] = (
                x_refs[g][pl.ds(c, cpp, stride=CMP_STRIDE), :])
    o_ref[...] = _dot(xc_ref[...], w_ref[...])


def _cmp_project(pool, page_table, w1):
    nb, n_pages = page_table.shape
    n_in = math.gcd(32, n_pages)
    cpp = PAGE // CMP_STRIDE
    w1r = w1.reshape(2, CMP_STRIDE * ATT_HD, ATT_HD)
    w1cat = jnp.concatenate([w1r[0], w1r[1]], axis=1).astype(BF16)

    def xspec(g):
        return pl.BlockSpec((None, PAGE, ATT_HD),
                            lambda b, h, j, pt: (pt[b * n_pages + j * n_in + g], 0, h))

    return pl.pallas_call(
        functools.partial(_cmp_proj_kernel, n_in=n_in),
        grid_spec=pltpu.PrefetchScalarGridSpec(
            num_scalar_prefetch=1,
            grid=(nb, KV_HEADS, n_pages // n_in),
            in_specs=[xspec(g) for g in range(n_in)]
            + [pl.BlockSpec((CMP_STRIDE * ATT_HD, 2 * ATT_HD), lambda b, h, j, pt: (0, 0))],
            out_specs=pl.BlockSpec((None, None, n_in * cpp, 2 * ATT_HD), lambda b, h, j, pt: (b, h, j, 0)),
            scratch_shapes=[pltpu.VMEM((n_in * cpp, CMP_STRIDE * ATT_HD), F32)]),
        out_shape=jax.ShapeDtypeStruct((nb, KV_HEADS, n_pages * cpp, 2 * ATT_HD), F32),
        compiler_params=_cp("arbitrary", "arbitrary", "arbitrary"),
    )(page_table.reshape(-1), *([pool] * n_in), w1cat)


def _cmp_mlp_kernel(pp_ref, pe_ref, w1_ref, w2_ref, gn_ref, o_ref, *, use_norm):
    pp = pp_ref[...]
    n = pp.shape[0]
    bias = _dot(pe_ref[...], w1_ref[...])[0:1]
    hid = pp[:, :ATT_HD] + pltpu.roll(pp[:, ATT_HD:], n - 1, 0) + bias
    out = _dot(jax.nn.gelu(hid), w2_ref[...])
    if use_norm:
        out = _rms(out, gn_ref[...])
    o_ref[...] = jnp.where(_iota((n, 1), 0) < n - 1, out, 0.0)


def _cmp_mlp(pp, pe, w1, w2, gn, use_norm):
    nb, _, n_chunk, _ = pp.shape
    flat = CMP_LEN * ATT_HD
    pe8 = jnp.broadcast_to(pe.reshape(1, flat), (8, flat))
    return pl.pallas_call(
        functools.partial(_cmp_mlp_kernel, use_norm=use_norm),
        grid=(nb, KV_HEADS),
        in_specs=[pl.BlockSpec((None, None, n_chunk, 2 * ATT_HD), lambda b, h: (b, h, 0, 0)),
                  pl.BlockSpec((8, flat), lambda b, h: (0, 0)),
                  pl.BlockSpec((flat, ATT_HD), lambda b, h: (0, 0)),
                  pl.BlockSpec((ATT_HD, ATT_HD), lambda b, h: (0, 0)),
                  pl.BlockSpec((1, ATT_HD), lambda b, h: (0, 0))],
        out_specs=pl.BlockSpec((None, None, n_chunk, ATT_HD), lambda b, h: (b, h, 0, 0)),
        out_shape=jax.ShapeDtypeStruct((nb, KV_HEADS, n_chunk, ATT_HD), F32),
        compiler_params=_cp("arbitrary", "arbitrary"),
    )(pp, pe8, w1.reshape(flat, ATT_HD).astype(BF16), w2.astype(BF16), gn.reshape(1, -1))


def _compress(pool, page_table, w1, w2, pe, gn, use_norm):
    return _cmp_mlp(_cmp_project(pool, page_table, w1), pe, w1, w2, gn, use_norm)


def _overlap(n_rows, n_cols, n_cmp):
    ci, sj = _iota((n_rows, n_cols), 0), _iota((n_rows, n_cols), 1)
    ov = ((ci * CMP_STRIDE < (sj + 1) * SLC_BLOCK) & (ci * CMP_STRIDE + CMP_LEN > sj * SLC_BLOCK)
          & (ci < n_cmp))
    return ov.astype(BF16)


def _select_blocks(imp, qpos, n_slc, top_n):
    blk = _iota(imp.shape, 1)
    cur = qpos // SLC_BLOCK
    forced = (blk == 0) | (blk == cur) | (blk == cur - 1)
    val = jnp.where(forced, FORCE, jnp.where(blk <= cur, imp, -FORCE))
    val = jnp.where(blk < n_slc, val, REMOVED)
    member = jnp.zeros(imp.shape, F32)
    for _ in range(top_n):
        m = jnp.max(val, axis=-1, keepdims=True)
        idx = jnp.min(jnp.where(val == m, blk, imp.shape[1]), axis=-1, keepdims=True)
        hit = blk == idx
        member = jnp.where(hit, 1.0, member)
        val = jnp.where(hit, REMOVED, val)
    return member


def _expand_blocks(member_bf16, first_key, n_keys):
    lanes = member_bf16.shape[1]
    jj, cc = _iota((lanes, n_keys), 0), _iota((lanes, n_keys), 1)
    sel = (jj == (first_key + cc) // SLC_BLOCK).astype(BF16)
    return jnp.dot(member_bf16, sel, preferred_element_type=F32)


def _online_update(carry, s, valid, v_bf16):
    m, l, acc = carry
    s = jnp.where(valid, s, NEG_INF)
    m_new = jnp.maximum(m, jnp.max(s, axis=-1, keepdims=True))
    p = jnp.where(valid, jnp.exp(s - m_new), 0.0)
    alpha = jnp.exp(m - m_new)
    l = l * alpha + jnp.sum(p, axis=-1, keepdims=True)
    acc = acc * alpha + jnp.dot(p.astype(BF16), v_bf16, preferred_element_type=F32)
    return m_new, l, acc


def _nsa_prompt_kernel(q_ref, gt_ref, ck_ref, cv_ref, ks_ref, vs_ref, kw_ref, vw_ref, o_ref,
                       *, n_cmp, n_slc, lanes, top_n, kt, band):
    i = pl.program_id(2)
    s0 = i * Q_BLOCK
    scale = ATT_HD ** -0.5
    rows = ATT_GROUP * Q_BLOCK
    qs = jnp.concatenate([q_ref[:, g * ATT_HD:(g + 1) * ATT_HD] for g in range(ATT_GROUP)],
                         axis=0).astype(BF16)
    qpos4 = s0 + (_iota((rows, 1), 0) & (Q_BLOCK - 1))
    qpos = s0 + _iota((Q_BLOCK, 1), 0)

    nc = ck_ref.shape[0]
    sc = _dot_nt(qs, ck_ref[...]) * scale
    cidx = _iota((1, nc), 1)
    p_cmp = _masked_softmax(sc, (cidx * CMP_STRIDE + CMP_LEN - 1 <= qpos4) & (cidx < n_cmp))
    o_cmp = _dot(p_cmp, cv_ref[...])
    p_sum = sum(p_cmp[g * Q_BLOCK:(g + 1) * Q_BLOCK] for g in range(ATT_GROUP))
    imp = _split_dot(p_sum, _overlap(nc, lanes, n_cmp), 2)
    member = _select_blocks(imp, qpos, n_slc, top_n).astype(BF16)
    member4 = jnp.concatenate([member] * ATT_GROUP, axis=0)

    def body(t, carry):
        k0 = pl.multiple_of(t * kt, kt)
        s = _dot_nt(qs, ks_ref[pl.ds(k0, kt), :]) * scale
        kpos = k0 + _iota((1, kt), 1)
        valid = (_expand_blocks(member4, k0, kt) > 0.5) & (kpos <= qpos4)
        return _online_update(carry, s, valid, vs_ref[pl.ds(k0, kt), :].astype(BF16))

    init = (jnp.full((rows, 1), NEG_INF, F32), jnp.zeros((rows, 1), F32), jnp.zeros((rows, ATT_HD), F32))
    _, l, acc = lax.fori_loop(0, (s0 + Q_BLOCK + kt - 1) // kt, body, init)
    o_slc = acc / jnp.maximum(l, 1e-30)

    start = pl.multiple_of(jnp.maximum(s0 + Q_BLOCK - band, 0), Q_BLOCK)
    wpos = start + _iota((1, band), 1)
    sw = _dot_nt(qs, kw_ref[pl.ds(start, band), :]) * scale
    p_win = _masked_softmax(sw, (wpos <= qpos4) & (wpos > qpos4 - WINDOW))
    o_win = _dot(p_win, vw_ref[pl.ds(start, band), :])

    gate = jax.nn.sigmoid(gt_ref[...])
    for g in range(ATT_GROUP):
        r = slice(g * Q_BLOCK, (g + 1) * Q_BLOCK)
        o_ref[:, g * ATT_HD:(g + 1) * ATT_HD] = (gate[:, 3 * g:3 * g + 1] * o_cmp[r]
                                                 + gate[:, 3 * g + 1:3 * g + 2] * o_slc[r]
                                                 + gate[:, 3 * g + 2:3 * g + 3] * o_win[r])


def _nsa_prompt(q, gates, ck, cv, ks, vs, kw, vw, b, t):
    nq = t // Q_BLOCK
    n_chunk = ck.shape[2]
    n_slc = -(-t // SLC_BLOCK)
    lanes = -(-n_slc // 128) * 128
    kt = 256 if t % 256 == 0 else Q_BLOCK
    band = min(WINDOW + Q_BLOCK, t)
    qspec = pl.BlockSpec((Q_BLOCK, ATT_GROUP * ATT_HD), lambda bb, h, i: (bb * nq + i, h))
    cspec = pl.BlockSpec((None, None, n_chunk, ATT_HD), lambda bb, h, i: (bb, h, 0, 0))
    kspec = pl.BlockSpec((t, ATT_HD), lambda bb, h, i: (bb, h))
    return pl.pallas_call(
        functools.partial(_nsa_prompt_kernel, n_cmp=n_chunk - 1, n_slc=n_slc, lanes=lanes,
                          top_n=min(SLC_TOPK, n_slc), kt=kt, band=band),
        grid=(b, KV_HEADS, nq),
        in_specs=[qspec, pl.BlockSpec((None, Q_BLOCK, 3 * ATT_GROUP), lambda bb, h, i: (h, bb * nq + i, 0)),
                  cspec, cspec, kspec, kspec, kspec, kspec],
        out_specs=qspec,
        out_shape=jax.ShapeDtypeStruct((b * t, ATT_WIDTH), F32),
        compiler_params=_cp("arbitrary", "arbitrary", "arbitrary"),
    )(q, gates, ck, cv, ks, vs, kw, vw)


def _nsa_sample_kernel(pt_ref, q_ref, gt_ref, ck_ref, cv_ref, *refs,
                       n_in, ts, tpad, n_past, n_cmp, n_slc, top_n, wbuf):
    kpages, vpages = refs[:n_in], refs[n_in:2 * n_in]
    (ksn_ref, vsn_ref, kwb_ref, vwb_ref, kwn_ref, vwn_ref, o_ref,
     mem_ref, ocmp_ref, m_ref, l_ref, acc_ref) = refs[2 * n_in:]
    j = pl.program_id(1)
    scale = ATT_HD ** -0.5
    rows = ATT_GROUP * tpad
    tok = _iota((rows, 1), 0) & (tpad - 1)
    qpos = n_past + tok
    nkeys = n_in * PAGE

    def q_of(h):
        return jnp.concatenate([q_ref[:, (h * ATT_GROUP + g) * ATT_HD:(h * ATT_GROUP + g + 1) * ATT_HD]
                                for g in range(ATT_GROUP)], axis=0).astype(BF16)

    def hcols(ref, h):
        return ref[:, h * ATT_HD:(h + 1) * ATT_HD]

    @pl.when(j == 0)
    def _():
        nc = ck_ref.shape[1]
        lanes = mem_ref.shape[2]
        cidx = _iota((1, nc), 1)
        for h in range(KV_HEADS):
            sc = _dot_nt(q_of(h), ck_ref[h]) * scale
            p_cmp = _masked_softmax(sc, (cidx * CMP_STRIDE + CMP_LEN - 1 <= qpos) & (cidx < n_cmp))
            ocmp_ref[h] = _dot(p_cmp, cv_ref[h])
            p_sum = sum(p_cmp[g * tpad:(g + 1) * tpad] for g in range(ATT_GROUP))
            imp = _split_dot(p_sum, _overlap(nc, lanes, n_cmp), 2)
            member = _select_blocks(imp, qpos[:tpad], n_slc, top_n)
            mem_ref[h] = jnp.concatenate([member] * ATT_GROUP, axis=0)
            m_ref[h] = jnp.full((rows, 1), NEG_INF, F32)
            l_ref[h] = jnp.zeros((rows, 1), F32)
            acc_ref[h] = jnp.zeros((rows, ATT_HD), F32)

    k0 = j * nkeys
    kpos = k0 + _iota((1, nkeys), 1)
    for h in range(KV_HEADS):
        kcat = jnp.concatenate([hcols(r, h) for r in kpages], axis=0).astype(BF16)
        vcat = jnp.concatenate([hcols(r, h) for r in vpages], axis=0).astype(BF16)
        s = _dot_nt(q_of(h), kcat) * scale
        valid = (_expand_blocks(mem_ref[h].astype(BF16), k0, nkeys) > 0.5) & (kpos <= qpos)
        m, l, acc = _online_update((m_ref[h], l_ref[h], acc_ref[h]), s, valid, vcat)
        m_ref[h], l_ref[h], acc_ref[h] = m, l, acc

    @pl.when(j == pl.num_programs(1) - 1)
    def _():
        gate = jax.nn.sigmoid(gt_ref[...])
        npad = ksn_ref.shape[0]
        ridx = _iota((1, npad), 1)
        new_ok = (ridx < ts) & (n_past + ridx <= qpos)
        cidx = _iota((1, wbuf + npad), 1)
        wpos = jnp.where(cidx < wbuf, n_past - wbuf + cidx, n_past + cidx - wbuf)
        w_ok = ((cidx - wbuf < ts) & (wpos <= qpos) & (wpos > qpos - WINDOW) & (wpos >= 0))
        for h in range(KV_HEADS):
            qh = q_of(h)
            blk = n_past // SLC_BLOCK
            s = _dot_nt(qh, hcols(ksn_ref, h)) * scale
            valid = (mem_ref[h][:, blk:blk + 1] > 0.5) & new_ok
            _, l, acc = _online_update((m_ref[h], l_ref[h], acc_ref[h]), s, valid,
                                       hcols(vsn_ref, h).astype(BF16))
            o_slc = acc / jnp.maximum(l, 1e-30)
            kw = jnp.concatenate([hcols(kwb_ref, h), hcols(kwn_ref, h)], axis=0)
            vw = jnp.concatenate([hcols(vwb_ref, h), hcols(vwn_ref, h)], axis=0)
            p_win = _masked_softmax(_dot_nt(qh, kw) * scale, w_ok)
            o_win = _dot(p_win, vw)
            o_cmp = ocmp_ref[h]
            for g in range(ATT_GROUP):
                r = slice(g * tpad, (g + 1) * tpad)
                c = 3 * (h * ATT_GROUP + g)
                hh = h * ATT_GROUP + g
                o_ref[:, hh * ATT_HD:(hh + 1) * ATT_HD] = (gate[:, c:c + 1] * o_cmp[r]
                                                           + gate[:, c + 1:c + 2] * o_slc[r]
                                                           + gate[:, c + 2:c + 3] * o_win[r])


def _nsa_sample(q, gates, ck, cv, pool_k, pool_v, page_table, ks_new, vs_new, kw_buf, vw_buf,
                kw_new, vw_new, ts):
    db, tpad, _ = q.shape
    n_pages = page_table.shape[1]
    n_past = n_pages * PAGE
    n_in = math.gcd(8, n_pages)
    n_chunk = ck.shape[2]
    n_slc = -(-(n_past + ts) // SLC_BLOCK)
    lanes = -(-n_slc // 128) * 128
    wbuf = kw_buf.shape[1]
    npad = ks_new.shape[1]
    rows = ATT_GROUP * tpad
    per_b = lambda s1, s2: pl.BlockSpec((None, s1, s2), lambda b, j, pt: (b, 0, 0))

    def pspec(g):
        return pl.BlockSpec((None, PAGE, KV_WIDTH), lambda b, j, pt: (pt[b * n_pages + j * n_in + g], 0, 0))

    cspec = pl.BlockSpec((None, KV_HEADS, n_chunk, ATT_HD), lambda b, j, pt: (b, 0, 0, 0))
    return pl.pallas_call(
        functools.partial(_nsa_sample_kernel, n_in=n_in, ts=ts, tpad=tpad, n_past=n_past,
                          n_cmp=n_chunk - 1, n_slc=n_slc, top_n=min(SLC_TOPK, n_slc), wbuf=wbuf),
        grid_spec=pltpu.PrefetchScalarGridSpec(
            num_scalar_prefetch=1,
            grid=(db, n_pages // n_in),
            in_specs=[per_b(tpad, ATT_WIDTH), per_b(tpad, 3 * ATT_HEADS), cspec, cspec]
            + [pspec(g) for g in range(n_in)] * 2
            + [per_b(npad, KV_WIDTH)] * 2 + [per_b(wbuf, KV_WIDTH)] * 2 + [per_b(npad, KV_WIDTH)] * 2,
            out_specs=per_b(tpad, ATT_WIDTH),
            scratch_shapes=[pltpu.VMEM((KV_HEADS, rows, lanes), F32),
                            pltpu.VMEM((KV_HEADS, rows, ATT_HD), F32),
                            pltpu.VMEM((KV_HEADS, rows, 1), F32),
                            pltpu.VMEM((KV_HEADS, rows, 1), F32),
                            pltpu.VMEM((KV_HEADS, rows, ATT_HD), F32)]),
        out_shape=jax.ShapeDtypeStruct((db, tpad, ATT_WIDTH), F32),
        compiler_params=_cp("arbitrary", "arbitrary"),
    )(page_table.reshape(-1), q, gates, ck, cv, *([pool_k] * n_in), *([pool_v] * n_in),
      ks_new, vs_new, kw_buf, vw_buf, kw_new, vw_new)


def _seg_mats():
    seg = (_iota((RW_WIDTH, 128), 0) // RW_HD == _iota((RW_WIDTH, 128), 1)).astype(BF16)
    seg_t = (_iota((128, RW_WIDTH), 1) // RW_HD == _iota((128, RW_WIDTH), 0)).astype(BF16)
    return seg, seg_t


def _rwkv_pre_kernel(f_ref, pf_ref, mu_ref, w0_ref, w2_ref, a0_ref, a2_ref, g2_ref, kk_w_ref, ka_w_ref,
                     rk_ref, r_ref, d_ref, k_ref, v_ref, kk_ref, kka_ref, g_ref, bonus_ref, carry_ref,
                     *, seq_len):
    tm = f_ref.shape[0]

    @pl.when(pl.program_id(0) == 0)
    def _():
        carry_ref[...] = jnp.zeros_like(carry_ref)

    feat = f_ref[...]
    row = _iota((tm, 1), 0)
    shifted = jnp.where(row == 0, carry_ref[...], pltpu.roll(feat, 1, 0))
    shifted = jnp.where((pl.program_id(0) * tm + row) % seq_len == 0, pf_ref[...], shifted)
    carry_ref[...] = feat[tm - 1:tm]
    mixed = feat + (shifted - feat) * mu_ref[...]
    w = RW_WIDTH
    r, k, v = mixed[:, :w], mixed[:, w:2 * w], mixed[:, 2 * w:3 * w]
    wd = mixed[:, 3 * w:3 * w + LORA_PAD]
    ad = mixed[:, 3 * w + LORA_PAD:3 * w + 2 * LORA_PAD]
    gd = mixed[:, 3 * w + 2 * LORA_PAD:]
    wlog = -jax.nn.softplus(-(w0_ref[...] + _dot(jnp.tanh(wd), w2_ref[...]))) - 0.5
    decay = jnp.exp(-jnp.exp(wlog))
    a = jax.nn.sigmoid(a0_ref[...] + _dot(ad, a2_ref[...]))
    g = _dot(jax.nn.sigmoid(gd), g2_ref[...])
    seg, seg_t = _seg_mats()
    kk = k * kk_w_ref[...]
    norm = jnp.maximum(jnp.sqrt(_split_dot(kk * kk, seg, 2)), 1e-12)
    kk = kk * _split_dot(1.0 / norm, seg_t, 2)
    k2 = k * (1.0 + (a - 1.0) * ka_w_ref[...])
    bonus = _split_dot(_split_dot(r * k2 * rk_ref[...], seg, 2), seg_t, 2) * v
    r_ref[...], d_ref[...], k_ref[...], v_ref[...] = r, decay, k2, v
    kk_ref[...], kka_ref[...], g_ref[...], bonus_ref[...] = kk, kk * a, g, bonus


def _rwkv_pre(feat, prev, seq_len, mu, w0, w2, a0, a2, g2, k_k, k_a, r_k):
    m = feat.shape[0]
    tm = min(256, m)
    vec = lambda n: pl.BlockSpec((1, n), lambda i: (0, 0))
    mat = lambda a, b: pl.BlockSpec((a, b), lambda i: (0, 0))
    row = pl.BlockSpec((tm, RW_WIDTH), lambda i: (i, 0))
    prev_blocks = prev.shape[0] // tm
    return pl.pallas_call(
        functools.partial(_rwkv_pre_kernel, seq_len=seq_len),
        grid=(m // tm,),
        in_specs=[pl.BlockSpec((tm, RW_PROJ_PAD), lambda i: (i, 0)),
                  pl.BlockSpec((tm, RW_PROJ_PAD), lambda i: (i % prev_blocks, 0)),
                  vec(RW_PROJ_PAD), vec(RW_WIDTH), mat(LORA_PAD, RW_WIDTH), vec(RW_WIDTH),
                  mat(LORA_PAD, RW_WIDTH), mat(GATE_LORA, RW_WIDTH), vec(RW_WIDTH), vec(RW_WIDTH),
                  vec(RW_WIDTH)],
        out_specs=[row] * 8,
        out_shape=[jax.ShapeDtypeStruct((m, RW_WIDTH), F32)] * 8,
        scratch_shapes=[pltpu.VMEM((1, RW_PROJ_PAD), F32)],
        compiler_params=_cp("arbitrary"),
    )(feat, prev, mu.reshape(1, -1), w0.reshape(1, -1), w2.astype(BF16), a0.reshape(1, -1),
      a2.astype(BF16), g2.astype(BF16), k_k.reshape(1, -1), k_a.reshape(1, -1), r_k.reshape(1, -1))


def _wkv_scan_kernel(r_ref, d_ref, k_ref, v_ref, kk_ref, kka_ref, s0_ref, o_ref, sf_ref, st_ref):
    c = pl.program_id(1)
    n_pairs = st_ref.shape[0]
    steps = r_ref.shape[0]

    @pl.when(c == 0)
    def _():
        st_ref[...] = s0_ref[...]

    diag = ((_iota((RW_HD, 128), 1) & (RW_HD - 1)) == _iota((RW_HD, 128), 0)).astype(F32)
    same_head = ((_iota((128, 128), 0) // RW_HD) == (_iota((128, 128), 1) // RW_HD)).astype(BF16)

    def run(base, n):
        for p in range(n_pairs):
            sl = slice(p * 128, (p + 1) * 128)
            r8, d8, k8, v8, kk8, kka8 = (ref[pl.ds(base, n), sl]
                                         for ref in (r_ref, d_ref, k_ref, v_ref, kk_ref, kka_ref))
            s = st_ref[p]
            outs = []
            for j in range(n):
                row = lambda z: z[j:j + 1]
                sa = -_split_dot(s * row(kk8), same_head, 2)
                v_col = _split_dot(jnp.broadcast_to(row(v8), (RW_HD, 128)) * diag, same_head, 2)
                s = s * row(d8) + sa * row(kka8) + v_col * row(k8)
                o_sum = _split_dot(s * row(r8), same_head, 2)
                outs.append(jnp.sum(o_sum * diag, axis=0, keepdims=True))
            st_ref[p] = s
            o_ref[pl.ds(base, n), sl] = jnp.concatenate(outs, axis=0)

    if steps % 8 == 0:
        def group(i, carry):
            run(pl.multiple_of(i * 8, 8), 8)
            return carry

        lax.fori_loop(0, steps // 8, group, 0)
    else:
        run(0, steps)

    @pl.when(c == pl.num_programs(1) - 1)
    def _():
        sf_ref[...] = st_ref[...]


def _wkv_scan(r, d, k, v, kk, kka, s0):
    b, t, _ = r.shape
    tc = math.gcd(256, t)
    n_pairs = RW_HEADS // 2
    seq = pl.BlockSpec((None, tc, RW_WIDTH), lambda bb, c: (bb, c, 0))
    st = pl.BlockSpec((None, n_pairs, RW_HD, 128), lambda bb, c: (bb, 0, 0, 0))
    return pl.pallas_call(
        _wkv_scan_kernel,
        grid=(b, t // tc),
        in_specs=[seq] * 6 + [st],
        out_specs=[seq, st],
        out_shape=[jax.ShapeDtypeStruct((b, t, RW_WIDTH), F32),
                   jax.ShapeDtypeStruct((b, n_pairs, RW_HD, 128), F32)],
        scratch_shapes=[pltpu.VMEM((n_pairs, RW_HD, 128), F32)],
        compiler_params=_cp("arbitrary", "arbitrary"),
    )(r, d, k, v, kk, kka, s0)


def _pack_state(s):
    b = s.shape[0]
    return s.reshape(b, RW_HEADS // 2, 2, RW_HD, RW_HD).transpose(0, 1, 3, 2, 4).reshape(
        b, RW_HEADS // 2, RW_HD, 2 * RW_HD)


def _unpack_state(s):
    b = s.shape[0]
    return s.reshape(b, RW_HEADS // 2, RW_HD, 2, RW_HD).transpose(0, 1, 3, 2, 4).reshape(
        b, RW_HEADS, RW_HD, RW_HD)


def _combine_kernel(oa_ref, ow_ref, bonus_ref, g_ref, lnw_ref, lnb_ref, an_ref, x_ref, w_ref, o_ref, cat_ref):
    @pl.when(pl.program_id(1) == 0)
    def _():
        for h in range(ATT_HEADS):
            sl = slice(h * ATT_HD, (h + 1) * ATT_HD)
            cat_ref[:, sl] = _rms(oa_ref[:, sl], an_ref[:, sl]).astype(BF16)
        seg, seg_t = _seg_mats()
        o = ow_ref[...]
        mean = _split_dot(_split_dot(o, seg, 2) * (1.0 / RW_HD), seg_t, 2)
        oc = o - mean
        var = _split_dot(oc * oc, seg, 2) * (1.0 / RW_HD)
        y = oc * _split_dot(lax.rsqrt(var + GN_EPS), seg_t, 2) * lnw_ref[...] + lnb_ref[...]
        cat_ref[:, ATT_WIDTH:] = ((y + bonus_ref[...]) * g_ref[...]).astype(BF16)

    o_ref[...] = x_ref[...] + jnp.dot(cat_ref[...], w_ref[...], preferred_element_type=F32)


def _combine(o_att, o_wkv, bonus, g, ln_w, ln_b, attn_norm, x, w_out_bf16):
    m = x.shape[0]
    tm = min(512, m)
    tn = 512
    rowa = pl.BlockSpec((tm, ATT_WIDTH), lambda i, j: (i, 0))
    roww = pl.BlockSpec((tm, RW_WIDTH), lambda i, j: (i, 0))
    vec = lambda n: pl.BlockSpec((1, n), lambda i, j: (0, 0))
    return pl.pallas_call(
        _combine_kernel,
        grid=(m // tm, D_MODEL // tn),
        in_specs=[rowa, roww, roww, roww, vec(RW_WIDTH), vec(RW_WIDTH), vec(ATT_WIDTH),
                  pl.BlockSpec((tm, tn), lambda i, j: (i, j)),
                  pl.BlockSpec((ATT_WIDTH + RW_WIDTH, tn), lambda i, j: (0, j))],
        out_specs=pl.BlockSpec((tm, tn), lambda i, j: (i, j)),
        out_shape=jax.ShapeDtypeStruct((m, D_MODEL), F32),
        scratch_shapes=[pltpu.VMEM((tm, ATT_WIDTH + RW_WIDTH), BF16)],
        compiler_params=_cp("arbitrary", "arbitrary"),
    )(o_att, o_wkv, bonus, g, ln_w.reshape(1, -1), ln_b.reshape(1, -1), attn_norm.reshape(1, -1), x,
      w_out_bf16)


def _peer_route_kernel(pq_ref, gq_ref, sk1_ref, sk2_ref, e_ref, gate_ref, *, n_keys):
    tm = pq_ref.shape[0]
    half = PEER_DK // 2
    k = PEER_TOPK
    lane = _iota((tm, 128), 1)
    klane = _iota((tm, n_keys), 1)
    clane = _iota((tm, k * k), 1)
    rep_a = (_iota((128, k * k), 1) // k == _iota((128, k * k), 0)).astype(BF16)
    rep_b = (_iota((128, k * k), 1) % k == _iota((128, k * k), 0)).astype(BF16)
    e_all = jnp.zeros((tm, 128), F32)
    gate_all = jnp.zeros((tm, 128), F32)

    def top_keys(s):
        val = jnp.zeros((tm, 128), F32)
        idx = jnp.zeros((tm, 128), F32)
        for j in range(k):
            m = jnp.max(s, axis=-1, keepdims=True)
            pick = jnp.min(jnp.where(s == m, klane, n_keys), axis=-1, keepdims=True)
            val = jnp.where(lane == j, m, val)
            idx = jnp.where(lane == j, pick.astype(F32), idx)
            s = jnp.where(klane == pick, REMOVED, s)
        return val, idx

    for h in range(PEER_HEADS):
        q = _rms(pq_ref[:, h * PEER_DK:(h + 1) * PEER_DK], gq_ref[...])
        v1, i1 = top_keys(_dot_nt(q[:, :half], sk1_ref[...]))
        v2, i2 = top_keys(_dot_nt(q[:, half:], sk2_ref[...]))
        cand = _split_dot(v1, rep_a, 3) + _split_dot(v2, rep_b, 3)
        ecand = (jnp.dot(i1.astype(BF16), rep_a, preferred_element_type=F32) * n_keys
                 + jnp.dot(i2.astype(BF16), rep_b, preferred_element_type=F32))
        best = jnp.full((tm, 128), REMOVED, F32)
        for j in range(k):
            m = jnp.max(cand, axis=-1, keepdims=True)
            pick = jnp.min(jnp.where(cand == m, clane, k * k), axis=-1, keepdims=True)
            hit = clane == pick
            e_j = jnp.max(jnp.where(hit, ecand, -1.0), axis=-1, keepdims=True)
            e_all = jnp.where(lane == h * k + j, e_j, e_all)
            best = jnp.where(lane == h * k + j, m, best)
            cand = jnp.where(hit, REMOVED, cand)
        mine = (lane >= h * k) & (lane < (h + 1) * k)
        ex = jnp.where(mine, jnp.exp(best - jnp.max(best, axis=-1, keepdims=True)), 0.0)
        gate_all = gate_all + ex / jnp.sum(ex, axis=-1, keepdims=True)

    e_ref[...] = e_all.astype(I32)
    gate_ref[...] = gate_all


def _peer_route(pq, q_norm, subkeys):
    m = pq.shape[0]
    tm = min(256, m)
    n_keys = subkeys.shape[1]
    half = PEER_DK // 2
    out = pl.BlockSpec((tm, 128), lambda i: (i, 0))
    return pl.pallas_call(
        functools.partial(_peer_route_kernel, n_keys=n_keys),
        grid=(m // tm,),
        in_specs=[pl.BlockSpec((tm, PEER_HEADS * PEER_DK), lambda i: (i, 0)),
                  pl.BlockSpec((1, PEER_DK), lambda i: (0, 0)),
                  pl.BlockSpec((n_keys, half), lambda i: (0, 0)),
                  pl.BlockSpec((n_keys, half), lambda i: (0, 0))],
        out_specs=[out, out],
        out_shape=[jax.ShapeDtypeStruct((m, 128), I32), jax.ShapeDtypeStruct((m, 128), F32)],
        compiler_params=_cp("arbitrary"),
    )(pq, q_norm.reshape(1, -1), subkeys[0], subkeys[1])


def _peer_expert_kernel(e_ref, gate_ref, x_ref, n2_ref, uv_hbm, o_ref, buf, sem, hb_ref, gt_ref):
    tt = x_ref.shape[0]
    n_sel = PEER_HEADS * PEER_TOPK
    x = x_ref[...]
    hb_ref[...] = _rms(x, n2_ref[...])
    gt_ref[...] = gate_ref[...].T
    tok_lane = _iota((n_sel, tt), 1)

    def row_copy(idx, slot, j):
        return pltpu.make_async_copy(uv_hbm.at[pl.ds(idx, 1)], buf.at[slot, pl.ds(j, 1)], sem.at[slot])

    def issue(t, slot):
        for j in range(n_sel):
            row_copy(e_ref[t, j], slot, j).start()

    def wait_all(slot):
        pltpu.make_async_copy(uv_hbm.at[pl.ds(0, n_sel)], buf.at[slot], sem.at[slot]).wait()

    issue(0, 0)

    def body(t, carry):
        slot = t & 1

        @pl.when(t + 1 < tt)
        def _():
            issue(t + 1, 1 - slot)

        wait_all(slot)
        h_row = hb_ref[pl.ds(t, 1), :]
        act = jnp.sum(buf[slot, :, :D_MODEL] * h_row, axis=-1, keepdims=True)
        gate = jnp.sum(jnp.where(tok_lane == t, gt_ref[...], 0.0), axis=-1, keepdims=True)
        w = gate * jax.nn.gelu(act)
        out = jnp.sum(buf[slot, :, D_MODEL:] * w, axis=0, keepdims=True)
        o_ref[pl.ds(t, 1), :] = x_ref[pl.ds(t, 1), :] + out
        return carry

    lax.fori_loop(0, tt, body, 0)


def _peer_experts(e, gate, x, norm2, uv):
    m = x.shape[0]
    tt = min(128, m)
    n_sel = PEER_HEADS * PEER_TOPK
    return pl.pallas_call(
        _peer_expert_kernel,
        grid=(m // tt,),
        in_specs=[pl.BlockSpec((tt, n_sel), lambda i: (i, 0), memory_space=pltpu.SMEM),
                  pl.BlockSpec((tt, n_sel), lambda i: (i, 0)),
                  pl.BlockSpec((tt, D_MODEL), lambda i: (i, 0)),
                  pl.BlockSpec((1, D_MODEL), lambda i: (0, 0)),
                  pl.BlockSpec(memory_space=pl.ANY)],
        out_specs=pl.BlockSpec((tt, D_MODEL), lambda i: (i, 0)),
        out_shape=jax.ShapeDtypeStruct((m, D_MODEL), F32),
        scratch_shapes=[pltpu.VMEM((2, n_sel, 2 * D_MODEL), F32),
                        pltpu.SemaphoreType.DMA((2,)),
                        pltpu.VMEM((tt, D_MODEL), F32),
                        pltpu.VMEM((n_sel, tt), F32)],
        compiler_params=_cp("arbitrary"),
    )(e, gate, x, norm2.reshape(1, -1), uv)


def _pad_rw(z):
    o1 = 3 * RW_WIDTH
    zeros = jnp.zeros(z.shape[:-1] + (LORA_PAD - DECAY_LORA,), z.dtype)
    return jnp.concatenate([z[..., :o1 + DECAY_LORA], zeros,
                            z[..., o1 + DECAY_LORA:o1 + DECAY_LORA + A_LORA], zeros,
                            z[..., o1 + DECAY_LORA + A_LORA:]], axis=-1)


def _unpad_rw(z):
    o1 = 3 * RW_WIDTH
    return jnp.concatenate([z[..., :o1 + DECAY_LORA], z[..., o1 + LORA_PAD:o1 + LORA_PAD + A_LORA],
                            z[..., o1 + 2 * LORA_PAD:]], axis=-1)


def _pad_rows(z, n):
    return jnp.pad(z, ((0, n - z.shape[0]), (0, 0)))


def kernel(x_prompt, x_sample, cache_k_cmp, cache_v_cmp, cache_k_slc, cache_v_slc, state_k_win, state_v_win, state_wkv, state_shift, page_table, norm1, w_in, q_norm, k_norm_slc, k_norm_win, k_norm_cmp, cmp_w1_k, cmp_w2_k, cmp_pe_k, cmp_w1_v, cmp_w2_v, cmp_pe_v, attn_out_norm, rw_mu, rw_w0, rw_w2, rw_a0, rw_a2, rw_g2, rw_k_k, rw_k_a, rw_r_k, rw_ln_w, rw_ln_b, w_out, norm2, peer_wq, peer_q_norm, peer_subkeys, peer_u, peer_v):
    assert x_prompt.shape[-1] == D_MODEL and cache_k_cmp.shape[0] == 1
    b, t, _ = x_prompt.shape
    db, ts, _ = x_sample.shape
    n_pages = page_table.shape[1]
    n_past = n_pages * PAGE
    n_pool = cache_k_cmp.shape[1]
    wbuf = state_k_win.shape[2]
    assert t % Q_BLOCK == 0 and n_past // CMP_STRIDE == (n_past + ts) // CMP_STRIDE and ts <= 8

    w_in0 = w_in[0]
    gates_w = w_in0[:, QKV_WIDTH:QKV_WIDTH + 3 * ATT_HEADS]
    w_qkvg = jnp.concatenate([w_in0[:, :QKV_WIDTH], gates_w,
                              jnp.zeros((D_MODEL, 128 - 3 * ATT_HEADS), F32)], axis=1).astype(BF16)
    w_rw = _pad_rw(w_in0[:, QKV_WIDTH + 3 * ATT_HEADS:]).astype(BF16)
    w_out_bf, wq_bf = w_out[0].astype(BF16), peer_wq[0].astype(BF16)
    uv = jnp.concatenate([peer_u[0], peer_v[0]], axis=1)
    mu_p = _pad_rw(rw_mu[0])
    w2_p, a2_p = _pad_rows(rw_w2[0], LORA_PAD), _pad_rows(rw_a2[0], LORA_PAD)
    pools = [c[0].reshape(n_pool, PAGE, KV_WIDTH) for c in (cache_k_cmp, cache_v_cmp, cache_k_slc, cache_v_slc)]

    def group(x, pos, seq_len, prev_feat, wkv0):
        nb, tl, _ = x.shape
        x2 = x.reshape(nb * tl, D_MODEL)
        proj = _norm_matmul(x2, norm1[0], w_qkvg, 896)
        feat = _norm_matmul(x2, norm1[0], w_rw, 896)
        q, kc, vc, ks, vs, kw, vw = _qkv_post(proj, pos, q_norm[0], k_norm_slc[0], k_norm_win[0])
        gates = proj[:, QKV_WIDTH:QKV_WIDTH + 3 * ATT_HEADS]
        r, d, k2, v, kk, kka, g, bonus = _rwkv_pre(feat, prev_feat, seq_len, mu_p, rw_w0[0], w2_p, rw_a0[0],
                                                   a2_p, rw_g2[0], rw_k_k[0], rw_k_a[0], rw_r_k[0])
        sq = lambda z: z.reshape(nb, tl, RW_WIDTH)
        o_wkv, s_fin = _wkv_scan(sq(r), sq(d), sq(k2), sq(v), sq(kk), sq(kka), _pack_state(wkv0))
        shift = _unpad_rw(feat.reshape(nb, tl, RW_PROJ_PAD)[:, -1])
        return x2, (q, gates, kc, vc, ks, vs, kw, vw), (o_wkv.reshape(nb * tl, RW_WIDTH), bonus, g), \
            _unpack_state(s_fin), shift

    def finish(x2, o_att, rw):
        x1 = _combine(o_att, *rw, rw_ln_w[0], rw_ln_b[0], attn_out_norm[0], x2, w_out_bf)
        pq = _norm_matmul(x1, norm2[0], wq_bf, 512)
        e, gate = _peer_route(pq, peer_q_norm[0], peer_subkeys[0])
        return _peer_experts(e, gate, x1, norm2[0], uv)

    def compress_kv(pool_k, pool_v, pt):
        ck = _compress(pool_k, pt, cmp_w1_k[0], cmp_w2_k[0], cmp_pe_k[0], k_norm_cmp[0], True)
        cv = _compress(pool_v, pt, cmp_w1_v[0], cmp_w2_v[0], cmp_pe_v[0], k_norm_cmp[0], False)
        return ck, cv

    kv5 = lambda z, nb, tl: z.reshape(1, nb, tl, KV_HEADS, ATT_HD)

    tm_pre = min(256, b * t)
    x2, (q, gates, kc, vc, ks, vs, kw, vw), rw, wkv_p, shift_p = group(
        x_prompt, jnp.arange(t), t, jnp.zeros((tm_pre, RW_PROJ_PAD), F32),
        jnp.zeros((b, RW_HEADS, RW_HD, RW_HD), F32))
    ident = jnp.arange(b * (t // PAGE), dtype=I32).reshape(b, t // PAGE)
    ck, cv = compress_kv(kc.reshape(-1, PAGE, KV_WIDTH), vc.reshape(-1, PAGE, KV_WIDTH), ident)
    gates_h = gates.reshape(b * t, KV_HEADS, 3 * ATT_GROUP).transpose(1, 0, 2)
    o_att = _nsa_prompt(q, gates_h, ck, cv, ks, vs, kw, vw, b, t)
    y_prompt = finish(x2, o_att, rw).reshape(b, t, D_MODEL)
    wp = min(WINDOW, t)
    p_state = (kv5(kc, b, t), kv5(vc, b, t), kv5(ks, b, t), kv5(vs, b, t),
               kv5(kw, b, t)[:, :, t - wp:], kv5(vw, b, t)[:, :, t - wp:], wkv_p[None], shift_p[None])

    prev = jnp.repeat(_pad_rw(state_shift[0]), ts, axis=0)
    x2, (q, gates, kc, vc, ks, vs, kw, vw), rw, wkv_s, shift_s = group(
        x_sample, jnp.tile(n_past + jnp.arange(ts), db), ts, prev, state_wkv[0])
    ck, cv = compress_kv(pools[0], pools[1], page_table)
    tpad = 8
    pad_t = lambda z, n: jnp.pad(z.reshape(db, ts, -1), ((0, 0), (0, n - ts), (0, 0)))
    kwb, vwb = state_k_win[0].reshape(db, wbuf, KV_WIDTH), state_v_win[0].reshape(db, wbuf, KV_WIDTH)
    o_att = _nsa_sample(pad_t(q, tpad), pad_t(gates, tpad), ck, cv, pools[2], pools[3], page_table,
                        pad_t(ks, PAGE), pad_t(vs, PAGE), kwb, vwb, pad_t(kw, PAGE), pad_t(vw, PAGE), ts)
    y_sample = finish(x2, o_att[:, :ts].reshape(db * ts, ATT_WIDTH), rw).reshape(db, ts, D_MODEL)
    kw_new = jnp.concatenate([kwb, kw.reshape(db, ts, KV_WIDTH)], axis=1)[:, -wbuf:]
    vw_new = jnp.concatenate([vwb, vw.reshape(db, ts, KV_WIDTH)], axis=1)[:, -wbuf:]
    s_state = (kv5(kc, db, ts), kv5(vc, db, ts), kv5(ks, db, ts), kv5(vs, db, ts),
               kv5(kw_new, db, wbuf), kv5(vw_new, db, wbuf), wkv_s[None], shift_s[None])
    return (y_prompt, y_sample, *p_state, *s_state)
```

```python
import functools
import math

import jax
import jax.numpy as jnp
from jax import lax
from jax.experimental import pallas as pl
from jax.experimental.pallas import tpu as pltpu

F32, BF16, I32 = jnp.float32, jnp.bfloat16, jnp.int32

D_MODEL = 2048
PAGE = 128
ATT_HEADS, ATT_HD, KV_HEADS, ATT_GROUP = 8, 128, 2, 4
ATT_WIDTH, KV_WIDTH = ATT_HEADS * ATT_HD, KV_HEADS * ATT_HD
RW_HD, RW_WIDTH, RW_HEADS = 64, 1024, 16
ROPE_DIM, ROPE_THETA = 32, 500000.0
CMP_LEN, CMP_STRIDE = 32, 16
SLC_BLOCK, SLC_TOPK, WINDOW, Q_BLOCK = 64, 16, 512, 128
DECAY_LORA, A_LORA, GATE_LORA = 96, 96, 256
LORA_PAD = 128
RW_PROJ = 3 * RW_WIDTH + DECAY_LORA + A_LORA + GATE_LORA
RW_PROJ_PAD = 3 * RW_WIDTH + 2 * LORA_PAD + GATE_LORA
QKV_WIDTH = ATT_WIDTH + 6 * KV_WIDTH
QKVG_PAD = QKV_WIDTH + 128
PEER_HEADS, PEER_DK, PEER_TOPK = 8, 256, 16
NORM_EPS, GN_EPS, NEG_INF, FORCE = 1e-6, 64e-5, -1e30, 1e4
REMOVED = -3e38
VMEM_LIMIT = 56 * 1024 * 1024


def _cp(*sem):
    return pltpu.CompilerParams(dimension_semantics=sem, vmem_limit_bytes=VMEM_LIMIT)


def _dot(a, b):
    return jnp.dot(a.astype(BF16), b.astype(BF16), preferred_element_type=F32)


def _dot_nt(a, b):
    return lax.dot_general(a.astype(BF16), b.astype(BF16), (((1,), (1,)), ((), ())),
                           preferred_element_type=F32)


def _split_dot(x, m_bf16, terms):
    out = None
    for _ in range(terms):
        hi = x.astype(BF16)
        part = jnp.dot(hi, m_bf16, preferred_element_type=F32)
        out = part if out is None else out + part
        x = x - hi.astype(F32)
    return out


def _rms(x, g):
    return x * lax.rsqrt(jnp.mean(x * x, axis=-1, keepdims=True) + NORM_EPS) * g


def _masked_softmax(s, mask):
    s = jnp.where(mask, s, NEG_INF)
    e = jnp.where(mask, jnp.exp(s - jnp.max(s, axis=-1, keepdims=True)), 0.0)
    return e / jnp.maximum(jnp.sum(e, axis=-1, keepdims=True), 1e-30)


def _iota(shape, dim):
    return lax.broadcasted_iota(I32, shape, dim)


def _norm_mm_kernel(x_ref, g_ref, w_ref, o_ref, xn_ref):
    @pl.when(pl.program_id(1) == 0)
    def _():
        xn_ref[...] = _rms(x_ref[...], g_ref[...]).astype(BF16)

    o_ref[...] = jnp.dot(xn_ref[...], w_ref[...], preferred_element_type=F32)


def _norm_matmul(x, g, w_bf16, tn):
    m, k = x.shape
    n = w_bf16.shape[1]
    tm = min(512, m)
    return pl.pallas_call(
        _norm_mm_kernel,
        grid=(m // tm, n // tn),
        in_specs=[pl.BlockSpec((tm, k), lambda i, j: (i, 0)),
                  pl.BlockSpec((1, k), lambda i, j: (0, 0)),
                  pl.BlockSpec((k, tn), lambda i, j: (0, j))],
        out_specs=pl.BlockSpec((tm, tn), lambda i, j: (i, j)),
        out_shape=jax.ShapeDtypeStruct((m, n), F32),
        scratch_shapes=[pltpu.VMEM((tm, k), BF16)],
        compiler_params=_cp("arbitrary", "arbitrary"),
    )(x, g.reshape(1, k), w_bf16)


def _qkv_post_kernel(p_ref, cos_ref, sa_ref, sb_ref, gq_ref, gs_ref, gw_ref,
                     q_ref, kc_ref, vc_ref, ks_ref, vs_ref, kw_ref, vw_ref):
    cos, sa, sb = cos_ref[...], sa_ref[...], sb_ref[...]

    def rope(x):
        return x * cos + pltpu.roll(x, ATT_HD - ROPE_DIM // 2, 1) * sa + pltpu.roll(x, ROPE_DIM // 2, 1) * sb

    def head(c):
        return p_ref[:, c * ATT_HD:(c + 1) * ATT_HD]

    for h in range(ATT_HEADS):
        q_ref[:, h * ATT_HD:(h + 1) * ATT_HD] = rope(_rms(head(h), gq_ref[...]))
    for h in range(KV_HEADS):
        sl = slice(h * ATT_HD, (h + 1) * ATT_HD)
        kc_ref[:, sl] = rope(head(8 + h))
        vc_ref[:, sl] = head(10 + h)
        ks_ref[:, sl] = rope(_rms(head(12 + h), gs_ref[...]))
        vs_ref[:, sl] = head(14 + h)
        kw_ref[:, sl] = rope(_rms(head(16 + h), gw_ref[...]))
        vw_ref[:, sl] = head(18 + h)


def _rope_tables(pos):
    half = ROPE_DIM // 2
    inv = ROPE_THETA ** (-jnp.arange(0, ROPE_DIM, 2, dtype=F32) / ROPE_DIM)
    ang = pos.astype(F32)[:, None] * inv[None, :]
    cos, sin = jnp.cos(ang), jnp.sin(ang)
    n = pos.shape[0]
    rest = ATT_HD - ROPE_DIM
    cos_t = jnp.concatenate([cos, cos, jnp.ones((n, rest), F32)], axis=1)
    sin_a = jnp.concatenate([-sin, jnp.zeros((n, half + rest), F32)], axis=1)
    sin_b = jnp.concatenate([jnp.zeros((n, half), F32), sin, jnp.zeros((n, rest), F32)], axis=1)
    return cos_t, sin_a, sin_b


def _qkv_post(proj, pos, q_norm, k_norm_slc, k_norm_win):
    m = proj.shape[0]
    p = pos.shape[0]
    tm = min(256, m, p)
    reps = p // tm
    tabs = _rope_tables(pos)
    row = lambda w: pl.BlockSpec((tm, w), lambda i: (i, 0))
    tab = pl.BlockSpec((tm, ATT_HD), lambda i: (i % reps, 0))
    gain = pl.BlockSpec((1, ATT_HD), lambda i: (0, 0))
    kv = jax.ShapeDtypeStruct((m, KV_WIDTH), F32)
    return pl.pallas_call(
        _qkv_post_kernel,
        grid=(m // tm,),
        in_specs=[row(QKV_WIDTH), tab, tab, tab, gain, gain, gain],
        out_specs=[row(ATT_WIDTH)] + [row(KV_WIDTH)] * 6,
        out_shape=[jax.ShapeDtypeStruct((m, ATT_WIDTH), F32)] + [kv] * 6,
        compiler_params=_cp("arbitrary"),
    )(proj, *tabs, q_norm.reshape(1, -1), k_norm_slc.reshape(1, -1), k_norm_win.reshape(1, -1))


def _cmp_proj_kernel(pt_ref, *refs, n_in):
    x_refs, w_ref, o_ref, xc_ref = refs[:n_in], refs[n_in], refs[n_in + 1], refs[n_in + 2]
    cpp = PAGE // CMP_STRIDE
    half = n_in * cpp
    for g in range(n_in):
        for h in range(KV_HEADS):
            for c in range(CMP_STRIDE):
                xc_ref[h * half + g * cpp:h * half + (g + 1) * cpp, c * ATT_HD:(c + 1) * ATT_HD] = (
                    x_refs[g][pl.ds(c, cpp, stride=CMP_STRIDE), h, :])
    pp = _dot(xc_ref[...], w_ref[...])
    o_ref[0] = pp[:half]
    o_ref[1] = pp[half:]


def _cmp_project(pool, page_table, w1):
    nb, n_pages = page_table.shape
    n_in = math.gcd(32, n_pages)
    cpp = PAGE // CMP_STRIDE
    w1r = w1.reshape(2, CMP_STRIDE * ATT_HD, ATT_HD)
    w1cat = jnp.concatenate([w1r[0], w1r[1]], axis=1).astype(BF16)

    def xspec(g):
        return pl.BlockSpec((None, None, PAGE, KV_HEADS, ATT_HD),
                            lambda b, j, pt: (0, pt[b * n_pages + j * n_in + g], 0, 0, 0))

    return pl.pallas_call(
        functools.partial(_cmp_proj_kernel, n_in=n_in),
        grid_spec=pltpu.PrefetchScalarGridSpec(
            num_scalar_prefetch=1,
            grid=(nb, n_pages // n_in),
            in_specs=[xspec(g) for g in range(n_in)]
            + [pl.BlockSpec((CMP_STRIDE * ATT_HD, 2 * ATT_HD), lambda b, j, pt: (0, 0))],
            out_specs=pl.BlockSpec((None, KV_HEADS, n_in * cpp, 2 * ATT_HD), lambda b, j, pt: (b, 0, j, 0)),
            scratch_shapes=[pltpu.VMEM((KV_HEADS * n_in * cpp, CMP_STRIDE * ATT_HD), F32)]),
        out_shape=jax.ShapeDtypeStruct((nb, KV_HEADS, n_pages * cpp, 2 * ATT_HD), F32),
        compiler_params=_cp("arbitrary", "arbitrary"),
    )(page_table.reshape(-1), *([pool] * n_in), w1cat)


def _cmp_mlp_kernel(pp_ref, pe_ref, w1_ref, w2_ref, gn_ref, o_ref, *, use_norm):
    pp = pp_ref[...]
    n = pp.shape[0]
    bias = _dot(pe_ref[...], w1_ref[...])[0:1]
    hid = pp[:, :ATT_HD] + pltpu.roll(pp[:, ATT_HD:], n - 1, 0) + bias
    out = _dot(jax.nn.gelu(hid), w2_ref[...])
    if use_norm:
        out = _rms(out, gn_ref[...])
    o_ref[...] = jnp.where(_iota((n, 1), 0) < n - 1, out, 0.0)


def _cmp_mlp(pp, pe, w1, w2, gn, use_norm):
    nb, _, n_chunk, _ = pp.shape
    flat = CMP_LEN * ATT_HD
    pe8 = jnp.broadcast_to(pe.reshape(1, flat), (8, flat))
    return pl.pallas_call(
        functools.partial(_cmp_mlp_kernel, use_norm=use_norm),
        grid=(nb, KV_HEADS),
        in_specs=[pl.BlockSpec((None, None, n_chunk, 2 * ATT_HD), lambda b, h: (b, h, 0, 0)),
                  pl.BlockSpec((8, flat), lambda b, h: (0, 0)),
                  pl.BlockSpec((flat, ATT_HD), lambda b, h: (0, 0)),
                  pl.BlockSpec((ATT_HD, ATT_HD), lambda b, h: (0, 0)),
                  pl.BlockSpec((1, ATT_HD), lambda b, h: (0, 0))],
        out_specs=pl.BlockSpec((None, None, n_chunk, ATT_HD), lambda b, h: (b, h, 0, 0)),
        out_shape=jax.ShapeDtypeStruct((nb, KV_HEADS, n_chunk, ATT_HD), F32),
        compiler_params=_cp("arbitrary", "arbitrary"),
    )(pp, pe8, w1.reshape(flat, ATT_HD).astype(BF16), w2.astype(BF16), gn.reshape(1, -1))


def _compress(pool, page_table, w1, w2, pe, gn, use_norm):
    return _cmp_mlp(_cmp_project(pool, page_table, w1), pe, w1, w2, gn, use_norm)


def _overlap(n_rows, n_cols, n_cmp):
    ci, sj = _iota((n_rows, n_cols), 0), _iota((n_rows, n_cols), 1)
    ov = ((ci * CMP_STRIDE < (sj + 1) * SLC_BLOCK) & (ci * CMP_STRIDE + CMP_LEN > sj * SLC_BLOCK)
          & (ci < n_cmp))
    return ov.astype(BF16)


def _select_blocks(imp, qpos, n_slc, top_n):
    blk = _iota(imp.shape, 1)
    cur = qpos // SLC_BLOCK
    forced = (blk == 0) | (blk == cur) | (blk == cur - 1)
    val = jnp.where(forced, FORCE, jnp.where(blk <= cur, imp, -FORCE))
    val = jnp.where(blk < n_slc, val, REMOVED)
    member = jnp.zeros(imp.shape, F32)
    for _ in range(top_n):
        m = jnp.max(val, axis=-1, keepdims=True)
        idx = jnp.min(jnp.where(val == m, blk, imp.shape[1]), axis=-1, keepdims=True)
        hit = blk == idx
        member = jnp.where(hit, 1.0, member)
        val = jnp.where(hit, REMOVED, val)
    return member


def _expand_blocks(member_bf16, first_key, n_keys):
    lanes = member_bf16.shape[1]
    jj, cc = _iota((lanes, n_keys), 0), _iota((lanes, n_keys), 1)
    sel = (jj == (first_key + cc) // SLC_BLOCK).astype(BF16)
    return jnp.dot(member_bf16, sel, preferred_element_type=F32)


def _online_update(carry, s, valid, v_bf16):
    m, l, acc = carry
    s = jnp.where(valid, s, NEG_INF)
    m_new = jnp.maximum(m, jnp.max(s, axis=-1, keepdims=True))
    p = jnp.where(valid, jnp.exp(s - m_new), 0.0)
    alpha = jnp.exp(m - m_new)
    l = l * alpha + jnp.sum(p, axis=-1, keepdims=True)
    acc = acc * alpha + jnp.dot(p.astype(BF16), v_bf16, preferred_element_type=F32)
    return m_new, l, acc


def _nsa_prompt_kernel(q_ref, gt_ref, ck_ref, cv_ref, ks_ref, vs_ref, kw_ref, vw_ref, o_ref,
                       *, n_cmp, n_slc, lanes, top_n, kt, band):
    i = pl.program_id(2)
    s0 = i * Q_BLOCK
    scale = ATT_HD ** -0.5
    rows = ATT_GROUP * Q_BLOCK
    qs = jnp.concatenate([q_ref[:, g * ATT_HD:(g + 1) * ATT_HD] for g in range(ATT_GROUP)],
                         axis=0).astype(BF16)
    qpos4 = s0 + (_iota((rows, 1), 0) & (Q_BLOCK - 1))
    qpos = s0 + _iota((Q_BLOCK, 1), 0)

    nc = ck_ref.shape[0]
    sc = _dot_nt(qs, ck_ref[...]) * scale
    cidx = _iota((1, nc), 1)
    p_cmp = _masked_softmax(sc, (cidx * CMP_STRIDE + CMP_LEN - 1 <= qpos4) & (cidx < n_cmp))
    o_cmp = _dot(p_cmp, cv_ref[...])
    p_sum = sum(p_cmp[g * Q_BLOCK:(g + 1) * Q_BLOCK] for g in range(ATT_GROUP))
    imp = _split_dot(p_sum, _overlap(nc, lanes, n_cmp), 2)
    member = _select_blocks(imp, qpos, n_slc, top_n).astype(BF16)
    member4 = jnp.concatenate([member] * ATT_GROUP, axis=0)

    def body(t, carry):
        k0 = pl.multiple_of(t * kt, kt)
        s = _dot_nt(qs, ks_ref[pl.ds(k0, kt), :]) * scale
        kpos = k0 + _iota((1, kt), 1)
        valid = (_expand_blocks(member4, k0, kt) > 0.5) & (kpos <= qpos4)
        return _online_update(carry, s, valid, vs_ref[pl.ds(k0, kt), :].astype(BF16))

    init = (jnp.full((rows, 1), NEG_INF, F32), jnp.zeros((rows, 1), F32), jnp.zeros((rows, ATT_HD), F32))
    _, l, acc = lax.fori_loop(0, (s0 + Q_BLOCK + kt - 1) // kt, body, init)
    o_slc = acc / jnp.maximum(l, 1e-30)

    start = pl.multiple_of(jnp.maximum(s0 + Q_BLOCK - band, 0), Q_BLOCK)
    wpos = start + _iota((1, band), 1)
    sw = _dot_nt(qs, kw_ref[pl.ds(start, band), :]) * scale
    p_win = _masked_softmax(sw, (wpos <= qpos4) & (wpos > qpos4 - WINDOW))
    o_win = _dot(p_win, vw_ref[pl.ds(start, band), :])

    gate = jax.nn.sigmoid(gt_ref[...])
    for g in range(ATT_GROUP):
        r = slice(g * Q_BLOCK, (g + 1) * Q_BLOCK)
        o_ref[:, g * ATT_HD:(g + 1) * ATT_HD] = (gate[:, 3 * g:3 * g + 1] * o_cmp[r]
                                                 + gate[:, 3 * g + 1:3 * g + 2] * o_slc[r]
                                                 + gate[:, 3 * g + 2:3 * g + 3] * o_win[r])


def _nsa_prompt(q, gates, ck, cv, ks, vs, kw, vw, b, t):
    nq = t // Q_BLOCK
    n_chunk = ck.shape[2]
    n_slc = -(-t // SLC_BLOCK)
    lanes = -(-n_slc // 128) * 128
    kt = 256 if t % 256 == 0 else Q_BLOCK
    band = min(WINDOW + Q_BLOCK, t)
    qspec = pl.BlockSpec((Q_BLOCK, ATT_GROUP * ATT_HD), lambda bb, h, i: (bb * nq + i, h))
    cspec = pl.BlockSpec((None, None, n_chunk, ATT_HD), lambda bb, h, i: (bb, h, 0, 0))
    kspec = pl.BlockSpec((t, ATT_HD), lambda bb, h, i: (bb, h))
    return pl.pallas_call(
        functools.partial(_nsa_prompt_kernel, n_cmp=n_chunk - 1, n_slc=n_slc, lanes=lanes,
                          top_n=min(SLC_TOPK, n_slc), kt=kt, band=band),
        grid=(b, KV_HEADS, nq),
        in_specs=[qspec, pl.BlockSpec((None, Q_BLOCK, 3 * ATT_GROUP), lambda bb, h, i: (h, bb * nq + i, 0)),
                  cspec, cspec, kspec, kspec, kspec, kspec],
        out_specs=qspec,
        out_shape=jax.ShapeDtypeStruct((b * t, ATT_WIDTH), F32),
        compiler_params=_cp("arbitrary", "arbitrary", "arbitrary"),
    )(q, gates, ck, cv, ks, vs, kw, vw)


def _nsa_sample_kernel(pt_ref, q_ref, gt_ref, ck_ref, cv_ref, *refs,
                       n_in, ts, tpad, n_past, n_cmp, n_slc, top_n, wbuf):
    kpages, vpages = refs[:n_in], refs[n_in:2 * n_in]
    (ksn_ref, vsn_ref, kwb_ref, vwb_ref, kwn_ref, vwn_ref, o_ref,
     mem_ref, ocmp_ref, m_ref, l_ref, acc_ref) = refs[2 * n_in:]
    j = pl.program_id(1)
    scale = ATT_HD ** -0.5
    rows = ATT_GROUP * tpad
    tok = _iota((rows, 1), 0) & (tpad - 1)
    qpos = n_past + tok
    nkeys = n_in * PAGE

    def q_of(h):
        return jnp.concatenate([q_ref[:, (h * ATT_GROUP + g) * ATT_HD:(h * ATT_GROUP + g + 1) * ATT_HD]
                                for g in range(ATT_GROUP)], axis=0).astype(BF16)

    def hcols(ref, h):
        return ref[:, h * ATT_HD:(h + 1) * ATT_HD]

    @pl.when(j == 0)
    def _():
        nc = ck_ref.shape[1]
        lanes = mem_ref.shape[2]
        cidx = _iota((1, nc), 1)
        for h in range(KV_HEADS):
            sc = _dot_nt(q_of(h), ck_ref[h]) * scale
            p_cmp = _masked_softmax(sc, (cidx * CMP_STRIDE + CMP_LEN - 1 <= qpos) & (cidx < n_cmp))
            ocmp_ref[h] = _dot(p_cmp, cv_ref[h])
            p_sum = sum(p_cmp[g * tpad:(g + 1) * tpad] for g in range(ATT_GROUP))
            imp = _split_dot(p_sum, _overlap(nc, lanes, n_cmp), 2)
            member = _select_blocks(imp, qpos[:tpad], n_slc, top_n)
            mem_ref[h] = jnp.concatenate([member] * ATT_GROUP, axis=0)
            m_ref[h] = jnp.full((rows, 1), NEG_INF, F32)
            l_ref[h] = jnp.zeros((rows, 1), F32)
            acc_ref[h] = jnp.zeros((rows, ATT_HD), F32)

    k0 = j * nkeys
    kpos = k0 + _iota((1, nkeys), 1)
    for h in range(KV_HEADS):
        kcat = jnp.concatenate([r[:, h, :] for r in kpages], axis=0).astype(BF16)
        vcat = jnp.concatenate([r[:, h, :] for r in vpages], axis=0).astype(BF16)
        s = _dot_nt(q_of(h), kcat) * scale
        valid = (_expand_blocks(mem_ref[h].astype(BF16), k0, nkeys) > 0.5) & (kpos <= qpos)
        m, l, acc = _online_update((m_ref[h], l_ref[h], acc_ref[h]), s, valid, vcat)
        m_ref[h], l_ref[h], acc_ref[h] = m, l, acc

    @pl.when(j == pl.num_programs(1) - 1)
    def _():
        gate = jax.nn.sigmoid(gt_ref[...])
        npad = ksn_ref.shape[0]
        ridx = _iota((1, npad), 1)
        new_ok = (ridx < ts) & (n_past + ridx <= qpos)
        cidx = _iota((1, wbuf + npad), 1)
        wpos = jnp.where(cidx < wbuf, n_past - wbuf + cidx, n_past + cidx - wbuf)
        w_ok = ((cidx - wbuf < ts) & (wpos <= qpos) & (wpos > qpos - WINDOW) & (wpos >= 0))
        for h in range(KV_HEADS):
            qh = q_of(h)
            blk = n_past // SLC_BLOCK
            s = _dot_nt(qh, hcols(ksn_ref, h)) * scale
            valid = (mem_ref[h][:, blk:blk + 1] > 0.5) & new_ok
            _, l, acc = _online_update((m_ref[h], l_ref[h], acc_ref[h]), s, valid,
                                       hcols(vsn_ref, h).astype(BF16))
            o_slc = acc / jnp.maximum(l, 1e-30)
            kw = jnp.concatenate([hcols(kwb_ref, h), hcols(kwn_ref, h)], axis=0)
            vw = jnp.concatenate([hcols(vwb_ref, h), hcols(vwn_ref, h)], axis=0)
            p_win = _masked_softmax(_dot_nt(qh, kw) * scale, w_ok)
            o_win = _dot(p_win, vw)
            o_cmp = ocmp_ref[h]
            for g in range(ATT_GROUP):
                r = slice(g * tpad, (g + 1) * tpad)
                c = 3 * (h * ATT_GROUP + g)
                hh = h * ATT_GROUP + g
                o_ref[:, hh * ATT_HD:(hh + 1) * ATT_HD] = (gate[:, c:c + 1] * o_cmp[r]
                                                           + gate[:, c + 1:c + 2] * o_slc[r]
                                                           + gate[:, c + 2:c + 3] * o_win[r])


def _nsa_sample(q, gates, ck, cv, pool_k, pool_v, page_table, ks_new, vs_new, kw_buf, vw_buf,
                kw_new, vw_new, ts):
    db, tpad, _ = q.shape
    n_pages = page_table.shape[1]
    n_past = n_pages * PAGE
    n_in = math.gcd(8, n_pages)
    n_chunk = ck.shape[2]
    n_slc = -(-(n_past + ts) // SLC_BLOCK)
    lanes = -(-n_slc // 128) * 128
    wbuf = kw_buf.shape[1]
    npad = ks_new.shape[1]
    rows = ATT_GROUP * tpad
    per_b = lambda s1, s2: pl.BlockSpec((None, s1, s2), lambda b, j, pt: (b, 0, 0))

    def pspec(g):
        return pl.BlockSpec((None, None, PAGE, KV_HEADS, ATT_HD),
                            lambda b, j, pt: (0, pt[b * n_pages + j * n_in + g], 0, 0, 0))

    cspec = pl.BlockSpec((None, KV_HEADS, n_chunk, ATT_HD), lambda b, j, pt: (b, 0, 0, 0))
    return pl.pallas_call(
        functools.partial(_nsa_sample_kernel, n_in=n_in, ts=ts, tpad=tpad, n_past=n_past,
                          n_cmp=n_chunk - 1, n_slc=n_slc, top_n=min(SLC_TOPK, n_slc), wbuf=wbuf),
        grid_spec=pltpu.PrefetchScalarGridSpec(
            num_scalar_prefetch=1,
            grid=(db, n_pages // n_in),
            in_specs=[per_b(tpad, ATT_WIDTH), per_b(tpad, 3 * ATT_HEADS), cspec, cspec]
            + [pspec(g) for g in range(n_in)] * 2
            + [per_b(npad, KV_WIDTH)] * 2 + [per_b(wbuf, KV_WIDTH)] * 2 + [per_b(npad, KV_WIDTH)] * 2,
            out_specs=per_b(tpad, ATT_WIDTH),
            scratch_shapes=[pltpu.VMEM((KV_HEADS, rows, lanes), F32),
                            pltpu.VMEM((KV_HEADS, rows, ATT_HD), F32),
                            pltpu.VMEM((KV_HEADS, rows, 1), F32),
                            pltpu.VMEM((KV_HEADS, rows, 1), F32),
                            pltpu.VMEM((KV_HEADS, rows, ATT_HD), F32)]),
        out_shape=jax.ShapeDtypeStruct((db, tpad, ATT_WIDTH), F32),
        compiler_params=_cp("arbitrary", "arbitrary"),
    )(page_table.reshape(-1), q, gates, ck, cv, *([pool_k] * n_in), *([pool_v] * n_in),
      ks_new, vs_new, kw_buf, vw_buf, kw_new, vw_new)


def _seg_mats():
    seg = (_iota((RW_WIDTH, 128), 0) // RW_HD == _iota((RW_WIDTH, 128), 1)).astype(BF16)
    seg_t = (_iota((128, RW_WIDTH), 1) // RW_HD == _iota((128, RW_WIDTH), 0)).astype(BF16)
    return seg, seg_t


def _rwkv_pre_kernel(f_ref, pf_ref, mu_ref, w0_ref, w2_ref, a0_ref, a2_ref, g2_ref, kk_w_ref, ka_w_ref,
                     rk_ref, r_ref, d_ref, k_ref, v_ref, kk_ref, kka_ref, g_ref, bonus_ref, carry_ref,
                     *, seq_len):
    tm = f_ref.shape[0]

    @pl.when(pl.program_id(0) == 0)
    def _():
        carry_ref[...] = jnp.zeros_like(carry_ref)

    feat = f_ref[...]
    row = _iota((tm, 1), 0)
    shifted = jnp.where(row == 0, carry_ref[...], pltpu.roll(feat, 1, 0))
    shifted = jnp.where((pl.program_id(0) * tm + row) % seq_len == 0, pf_ref[...], shifted)
    carry_ref[...] = feat[tm - 1:tm]
    mixed = feat + (shifted - feat) * mu_ref[...]
    w = RW_WIDTH
    r, k, v = mixed[:, :w], mixed[:, w:2 * w], mixed[:, 2 * w:3 * w]
    wd = mixed[:, 3 * w:3 * w + LORA_PAD]
    ad = mixed[:, 3 * w + LORA_PAD:3 * w + 2 * LORA_PAD]
    gd = mixed[:, 3 * w + 2 * LORA_PAD:]
    wlog = -jax.nn.softplus(-(w0_ref[...] + _dot(jnp.tanh(wd), w2_ref[...]))) - 0.5
    decay = jnp.exp(-jnp.exp(wlog))
    a = jax.nn.sigmoid(a0_ref[...] + _dot(ad, a2_ref[...]))
    g = _dot(jax.nn.sigmoid(gd), g2_ref[...])
    seg, seg_t = _seg_mats()
    kk = k * kk_w_ref[...]
    norm = jnp.maximum(jnp.sqrt(_split_dot(kk * kk, seg, 2)), 1e-12)
    kk = kk * _split_dot(1.0 / norm, seg_t, 2)
    k2 = k * (1.0 + (a - 1.0) * ka_w_ref[...])
    bonus = _split_dot(_split_dot(r * k2 * rk_ref[...], seg, 2), seg_t, 2) * v
    r_ref[...], d_ref[...], k_ref[...], v_ref[...] = r, decay, k2, v
    kk_ref[...], kka_ref[...], g_ref[...], bonus_ref[...] = kk, kk * a, g, bonus


def _rwkv_pre(feat, prev, seq_len, mu, w0, w2, a0, a2, g2, k_k, k_a, r_k):
    m = feat.shape[0]
    tm = min(256, m)
    vec = lambda n: pl.BlockSpec((1, n), lambda i: (0, 0))
    mat = lambda a, b: pl.BlockSpec((a, b), lambda i: (0, 0))
    row = pl.BlockSpec((tm, RW_WIDTH), lambda i: (i, 0))
    prev_blocks = prev.shape[0] // tm
    return pl.pallas_call(
        functools.partial(_rwkv_pre_kernel, seq_len=seq_len),
        grid=(m // tm,),
        in_specs=[pl.BlockSpec((tm, RW_PROJ_PAD), lambda i: (i, 0)),
                  pl.BlockSpec((tm, RW_PROJ_PAD), lambda i: (i % prev_blocks, 0)),
                  vec(RW_PROJ_PAD), vec(RW_WIDTH), mat(LORA_PAD, RW_WIDTH), vec(RW_WIDTH),
                  mat(LORA_PAD, RW_WIDTH), mat(GATE_LORA, RW_WIDTH), vec(RW_WIDTH), vec(RW_WIDTH),
                  vec(RW_WIDTH)],
        out_specs=[row] * 8,
        out_shape=[jax.ShapeDtypeStruct((m, RW_WIDTH), F32)] * 8,
        scratch_shapes=[pltpu.VMEM((1, RW_PROJ_PAD), F32)],
        compiler_params=_cp("arbitrary"),
    )(feat, prev, mu.reshape(1, -1), w0.reshape(1, -1), w2.astype(BF16), a0.reshape(1, -1),
      a2.astype(BF16), g2.astype(BF16), k_k.reshape(1, -1), k_a.reshape(1, -1), r_k.reshape(1, -1))


def _wkv_scan_kernel(r_ref, d_ref, k_ref, v_ref, kk_ref, kka_ref, s0_ref, o_ref, sf_ref, st_ref):
    c = pl.program_id(1)
    bg, steps = r_ref.shape[0], r_ref.shape[1]
    chains = [(b, p) for b in range(bg) for p in range(RW_HEADS // 2)]

    @pl.when(c == 0)
    def _():
        st_ref[...] = s0_ref[...]

    diag = ((_iota((RW_HD, 128), 1) & (RW_HD - 1)) == _iota((RW_HD, 128), 0)).astype(F32)
    same_head = ((_iota((128, 128), 0) // RW_HD) == (_iota((128, 128), 1) // RW_HD)).astype(BF16)
    same_head2 = jnp.concatenate([same_head, same_head], axis=0)

    def split(x):
        hi = x.astype(BF16)
        return hi, x - hi.astype(F32)

    def blk(z, i):
        return z[i * RW_HD:(i + 1) * RW_HD]

    def run(base, n):
        tiles = [[ref[b, pl.ds(base, n), :] for b in range(bg)]
                 for ref in (r_ref, d_ref, k_ref, v_ref, kk_ref, kka_ref)]
        s = [st_ref[b, p] for b, p in chains]
        outs = [[] for _ in chains]
        for j in range(n):
            r_, d_, k_, v_, kk_, kka_ = ([t[b][j:j + 1, p * 128:(p + 1) * 128] for b, p in chains]
                                         for t in tiles)
            lhs = []
            for i in range(len(chains)):
                hi, lo = split(s[i] * kk_[i])
                lhs.append(jnp.concatenate([hi, lo.astype(BF16)], axis=1))
            s_kk = jnp.dot(jnp.concatenate(lhs, axis=0), same_head2, preferred_element_type=F32)
            lhs = []
            for i in range(len(chains)):
                hi, lo = split(v_[i])
                lhs.append(jnp.concatenate(
                    [(jnp.broadcast_to(hi.astype(F32), (RW_HD, 128)) * diag).astype(BF16),
                     (jnp.broadcast_to(lo, (RW_HD, 128)) * diag).astype(BF16)], axis=1))
            v_col = jnp.dot(jnp.concatenate(lhs, axis=0), same_head2, preferred_element_type=F32)
            s = [s[i] * d_[i] - blk(s_kk, i) * kka_[i] + blk(v_col, i) * k_[i] for i in range(len(chains))]
            o_sum = jnp.dot(jnp.concatenate([(s[i] * r_[i]).astype(BF16) for i in range(len(chains))], axis=0),
                            same_head, preferred_element_type=F32)
            for i in range(len(chains)):
                outs[i].append(jnp.sum(blk(o_sum, i) * diag, axis=0, keepdims=True))
        for i, (b, p) in enumerate(chains):
            st_ref[b, p] = s[i]
            o_ref[b, pl.ds(base, n), p * 128:(p + 1) * 128] = jnp.concatenate(outs[i], axis=0)

    if steps % 8 == 0:
        def group(i, carry):
            run(pl.multiple_of(i * 8, 8), 8)
            return carry

        lax.fori_loop(0, steps // 8, group, 0)
    else:
        run(0, steps)

    @pl.when(c == pl.num_programs(1) - 1)
    def _():
        sf_ref[...] = st_ref[...]


def _wkv_scan(r, d, k, v, kk, kka, s0):
    b, t, _ = r.shape
    tc = math.gcd(64, t)
    bg = math.gcd(2, b)
    n_pairs = RW_HEADS // 2
    seq = pl.BlockSpec((bg, tc, RW_WIDTH), lambda bb, c: (bb, c, 0))
    st = pl.BlockSpec((bg, n_pairs, RW_HD, 128), lambda bb, c: (bb, 0, 0, 0))
    return pl.pallas_call(
        _wkv_scan_kernel,
        grid=(b // bg, t // tc),
        in_specs=[seq] * 6 + [st],
        out_specs=[seq, st],
        out_shape=[jax.ShapeDtypeStruct((b, t, RW_WIDTH), F32),
                   jax.ShapeDtypeStruct((b, n_pairs, RW_HD, 128), F32)],
        scratch_shapes=[pltpu.VMEM((bg, n_pairs, RW_HD, 128), F32)],
        compiler_params=_cp("arbitrary", "arbitrary"),
    )(r, d, k, v, kk, kka, s0)


def _pack_state(s):
    b = s.shape[0]
    return s.reshape(b, RW_HEADS // 2, 2, RW_HD, RW_HD).transpose(0, 1, 3, 2, 4).reshape(
        b, RW_HEADS // 2, RW_HD, 2 * RW_HD)


def _unpack_state(s):
    b = s.shape[0]
    return s.reshape(b, RW_HEADS // 2, RW_HD, 2, RW_HD).transpose(0, 1, 3, 2, 4).reshape(
        b, RW_HEADS, RW_HD, RW_HD)


def _combine_kernel(oa_ref, ow_ref, bonus_ref, g_ref, lnw_ref, lnb_ref, an_ref, x_ref, w_ref, o_ref, cat_ref):
    @pl.when(pl.program_id(1) == 0)
    def _():
        for h in range(ATT_HEADS):
            sl = slice(h * ATT_HD, (h + 1) * ATT_HD)
            cat_ref[:, sl] = _rms(oa_ref[:, sl], an_ref[:, sl]).astype(BF16)
        seg, seg_t = _seg_mats()
        o = ow_ref[...]
        mean = _split_dot(_split_dot(o, seg, 2) * (1.0 / RW_HD), seg_t, 2)
        oc = o - mean
        var = _split_dot(oc * oc, seg, 2) * (1.0 / RW_HD)
        y = oc * _split_dot(lax.rsqrt(var + GN_EPS), seg_t, 2) * lnw_ref[...] + lnb_ref[...]
        cat_ref[:, ATT_WIDTH:] = ((y + bonus_ref[...]) * g_ref[...]).astype(BF16)

    o_ref[...] = x_ref[...] + jnp.dot(cat_ref[...], w_ref[...], preferred_element_type=F32)


def _combine(o_att, o_wkv, bonus, g, ln_w, ln_b, attn_norm, x, w_out_bf16):
    m = x.shape[0]
    tm = min(512, m)
    tn = 512
    rowa = pl.BlockSpec((tm, ATT_WIDTH), lambda i, j: (i, 0))
    roww = pl.BlockSpec((tm, RW_WIDTH), lambda i, j: (i, 0))
    vec = lambda n: pl.BlockSpec((1, n), lambda i, j: (0, 0))
    return pl.pallas_call(
        _combine_kernel,
        grid=(m // tm, D_MODEL // tn),
        in_specs=[rowa, roww, roww, roww, vec(RW_WIDTH), vec(RW_WIDTH), vec(ATT_WIDTH),
                  pl.BlockSpec((tm, tn), lambda i, j: (i, j)),
                  pl.BlockSpec((ATT_WIDTH + RW_WIDTH, tn), lambda i, j: (0, j))],
        out_specs=pl.BlockSpec((tm, tn), lambda i, j: (i, j)),
        out_shape=jax.ShapeDtypeStruct((m, D_MODEL), F32),
        scratch_shapes=[pltpu.VMEM((tm, ATT_WIDTH + RW_WIDTH), BF16)],
        compiler_params=_cp("arbitrary", "arbitrary"),
    )(o_att, o_wkv, bonus, g, ln_w.reshape(1, -1), ln_b.reshape(1, -1), attn_norm.reshape(1, -1), x,
      w_out_bf16)


def _peer_route_kernel(pq_ref, gq_ref, sk1_ref, sk2_ref, e_ref, gate_ref, *, n_keys):
    tm = pq_ref.shape[0]
    half = PEER_DK // 2
    k = PEER_TOPK
    row = _iota((128, tm), 0)
    krow = _iota((n_keys, tm), 0)
    crow = _iota((k * k, tm), 0)
    rep_a = (_iota((k * k, 128), 0) // k == _iota((k * k, 128), 1)).astype(BF16)
    rep_b = (_iota((k * k, 128), 0) % k == _iota((k * k, 128), 1)).astype(BF16)
    e_all = jnp.zeros((128, tm), F32)
    gate_all = jnp.zeros((128, tm), F32)

    def spread(rep, x, terms):
        out = None
        for _ in range(terms):
            hi = x.astype(BF16)
            part = jnp.dot(rep, hi, preferred_element_type=F32)
            out = part if out is None else out + part
            x = x - hi.astype(F32)
        return out

    def top_keys(s):
        val = jnp.zeros((128, tm), F32)
        idx = jnp.zeros((128, tm), F32)
        for j in range(k):
            m = jnp.max(s, axis=0, keepdims=True)
            pick = jnp.min(jnp.where(s == m, krow, n_keys), axis=0, keepdims=True)
            val = jnp.where(row == j, m, val)
            idx = jnp.where(row == j, pick.astype(F32), idx)
            s = jnp.where(krow == pick, REMOVED, s)
        return val, idx

    for h in range(PEER_HEADS):
        q = _rms(pq_ref[:, h * PEER_DK:(h + 1) * PEER_DK], gq_ref[...])
        v1, i1 = top_keys(_dot_nt(sk1_ref[...], q[:, :half]))
        v2, i2 = top_keys(_dot_nt(sk2_ref[...], q[:, half:]))
        cand = spread(rep_a, v1, 3) + spread(rep_b, v2, 3)
        ecand = spread(rep_a, i1, 1) * n_keys + spread(rep_b, i2, 1)
        best = jnp.full((128, tm), REMOVED, F32)
        for j in range(k):
            m = jnp.max(cand, axis=0, keepdims=True)
            pick = jnp.min(jnp.where(cand == m, crow, k * k), axis=0, keepdims=True)
            hit = crow == pick
            e_j = jnp.max(jnp.where(hit, ecand, -1.0), axis=0, keepdims=True)
            e_all = jnp.where(row == h * k + j, e_j, e_all)
            best = jnp.where(row == h * k + j, m, best)
            cand = jnp.where(hit, REMOVED, cand)
        mine = (row >= h * k) & (row < (h + 1) * k)
        ex = jnp.where(mine, jnp.exp(best - jnp.max(best, axis=0, keepdims=True)), 0.0)
        gate_all = gate_all + ex / jnp.sum(ex, axis=0, keepdims=True)

    e_ref[...] = e_all.astype(I32)
    gate_ref[...] = gate_all


def _peer_route(pq, q_norm, subkeys):
    m = pq.shape[0]
    tm = min(256, m)
    n_keys = subkeys.shape[1]
    half = PEER_DK // 2
    out = pl.BlockSpec((128, tm), lambda i: (0, i))
    return pl.pallas_call(
        functools.partial(_peer_route_kernel, n_keys=n_keys),
        grid=(m // tm,),
        in_specs=[pl.BlockSpec((tm, PEER_HEADS * PEER_DK), lambda i: (i, 0)),
                  pl.BlockSpec((1, PEER_DK), lambda i: (0, 0)),
                  pl.BlockSpec((n_keys, half), lambda i: (0, 0)),
                  pl.BlockSpec((n_keys, half), lambda i: (0, 0))],
        out_specs=[out, out],
        out_shape=[jax.ShapeDtypeStruct((128, m), I32), jax.ShapeDtypeStruct((128, m), F32)],
        compiler_params=_cp("arbitrary"),
    )(pq, q_norm.reshape(1, -1), subkeys[0], subkeys[1])


def _peer_expert_kernel(e_ref, gate_ref, x_ref, n2_ref, uv_hbm, o_ref, buf, sem, hb_ref):
    tt = x_ref.shape[0]
    n_sel = PEER_HEADS * PEER_TOPK
    x = x_ref[...]
    hb_ref[...] = _rms(x, n2_ref[...])
    tok_lane = _iota((n_sel, tt), 1)

    def row_copy(idx, slot, j):
        return pltpu.make_async_copy(uv_hbm.at[pl.ds(idx, 1)], buf.at[slot, pl.ds(j, 1)], sem.at[slot])

    def issue(t, slot):
        for j in range(n_sel):
            row_copy(e_ref[j, t], slot, j).start()

    def wait_all(slot):
        pltpu.make_async_copy(uv_hbm.at[pl.ds(0, n_sel)], buf.at[slot], sem.at[slot]).wait()

    issue(0, 0)

    def body(t, carry):
        slot = t & 1

        @pl.when(t + 1 < tt)
        def _():
            issue(t + 1, 1 - slot)

        wait_all(slot)
        h_row = hb_ref[pl.ds(t, 1), :]
        act = jnp.sum(buf[slot, :, :D_MODEL] * h_row, axis=-1, keepdims=True)
        gate = jnp.sum(jnp.where(tok_lane == t, gate_ref[...], 0.0), axis=-1, keepdims=True)
        w = gate * jax.nn.gelu(act)
        out = jnp.sum(buf[slot, :, D_MODEL:] * w, axis=0, keepdims=True)
        o_ref[pl.ds(t, 1), :] = x_ref[pl.ds(t, 1), :] + out
        return carry

    lax.fori_loop(0, tt, body, 0)


def _peer_experts(e, gate, x, norm2, uv):
    m = x.shape[0]
    tt = min(128, m)
    n_sel = PEER_HEADS * PEER_TOPK
    return pl.pallas_call(
        _peer_expert_kernel,
        grid=(m // tt,),
        in_specs=[pl.BlockSpec((n_sel, tt), lambda i: (0, i), memory_space=pltpu.SMEM),
                  pl.BlockSpec((n_sel, tt), lambda i: (0, i)),
                  pl.BlockSpec((tt, D_MODEL), lambda i: (i, 0)),
                  pl.BlockSpec((1, D_MODEL), lambda i: (0, 0)),
                  pl.BlockSpec(memory_space=pl.ANY)],
        out_specs=pl.BlockSpec((tt, D_MODEL), lambda i: (i, 0)),
        out_shape=jax.ShapeDtypeStruct((m, D_MODEL), F32),
        scratch_shapes=[pltpu.VMEM((2, n_sel, 2 * D_MODEL), F32),
                        pltpu.SemaphoreType.DMA((2,)),
                        pltpu.VMEM((tt, D_MODEL), F32)],
        compiler_params=_cp("arbitrary"),
    )(e, gate, x, norm2.reshape(1, -1), uv)


def _pad_rw(z):
    o1 = 3 * RW_WIDTH
    zeros = jnp.zeros(z.shape[:-1] + (LORA_PAD - DECAY_LORA,), z.dtype)
    return jnp.concatenate([z[..., :o1 + DECAY_LORA], zeros,
                            z[..., o1 + DECAY_LORA:o1 + DECAY_LORA + A_LORA], zeros,
                            z[..., o1 + DECAY_LORA + A_LORA:]], axis=-1)


def _unpad_rw(z):
    o1 = 3 * RW_WIDTH
    return jnp.concatenate([z[..., :o1 + DECAY_LORA], z[..., o1 + LORA_PAD:o1 + LORA_PAD + A_LORA],
                            z[..., o1 + 2 * LORA_PAD:]], axis=-1)


def _pad_rows(z, n):
    return jnp.pad(z, ((0, n - z.shape[0]), (0, 0)))


def kernel(x_prompt, x_sample, cache_k_cmp, cache_v_cmp, cache_k_slc, cache_v_slc, state_k_win, state_v_win, state_wkv, state_shift, page_table, norm1, w_in, q_norm, k_norm_slc, k_norm_win, k_norm_cmp, cmp_w1_k, cmp_w2_k, cmp_pe_k, cmp_w1_v, cmp_w2_v, cmp_pe_v, attn_out_norm, rw_mu, rw_w0, rw_w2, rw_a0, rw_a2, rw_g2, rw_k_k, rw_k_a, rw_r_k, rw_ln_w, rw_ln_b, w_out, norm2, peer_wq, peer_q_norm, peer_subkeys, peer_u, peer_v):
    assert x_prompt.shape[-1] == D_MODEL and cache_k_cmp.shape[0] == 1
    b, t, _ = x_prompt.shape
    db, ts, _ = x_sample.shape
    n_pages = page_table.shape[1]
    n_past = n_pages * PAGE
    n_pool = cache_k_cmp.shape[1]
    wbuf = state_k_win.shape[2]
    assert t % Q_BLOCK == 0 and n_past // CMP_STRIDE == (n_past + ts) // CMP_STRIDE and ts <= 8

    w_in0 = w_in[0]
    gates_w = w_in0[:, QKV_WIDTH:QKV_WIDTH + 3 * ATT_HEADS]
    w_qkvg = jnp.concatenate([w_in0[:, :QKV_WIDTH], gates_w,
                              jnp.zeros((D_MODEL, 128 - 3 * ATT_HEADS), F32)], axis=1).astype(BF16)
    w_rw = _pad_rw(w_in0[:, QKV_WIDTH + 3 * ATT_HEADS:]).astype(BF16)
    w_out_bf, wq_bf = w_out[0].astype(BF16), peer_wq[0].astype(BF16)
    uv = jnp.concatenate([peer_u[0], peer_v[0]], axis=1)
    mu_p = _pad_rw(rw_mu[0])
    w2_p, a2_p = _pad_rows(rw_w2[0], LORA_PAD), _pad_rows(rw_a2[0], LORA_PAD)
    pools = (cache_k_cmp, cache_v_cmp, cache_k_slc, cache_v_slc)

    def group(x, pos, seq_len, prev_feat, wkv0):
        nb, tl, _ = x.shape
        x2 = x.reshape(nb * tl, D_MODEL)
        proj = _norm_matmul(x2, norm1[0], w_qkvg, 896)
        feat = _norm_matmul(x2, norm1[0], w_rw, 896)
        q, kc, vc, ks, vs, kw, vw = _qkv_post(proj, pos, q_norm[0], k_norm_slc[0], k_norm_win[0])
        gates = proj[:, QKV_WIDTH:QKV_WIDTH + 3 * ATT_HEADS]
        r, d, k2, v, kk, kka, g, bonus = _rwkv_pre(feat, prev_feat, seq_len, mu_p, rw_w0[0], w2_p, rw_a0[0],
                                                   a2_p, rw_g2[0], rw_k_k[0], rw_k_a[0], rw_r_k[0])
        sq = lambda z: z.reshape(nb, tl, RW_WIDTH)
        o_wkv, s_fin = _wkv_scan(sq(r), sq(d), sq(k2), sq(v), sq(kk), sq(kka), _pack_state(wkv0))
        shift = _unpad_rw(feat.reshape(nb, tl, RW_PROJ_PAD)[:, -1])
        return x2, (q, gates, kc, vc, ks, vs, kw, vw), (o_wkv.reshape(nb * tl, RW_WIDTH), bonus, g), \
            _unpack_state(s_fin), shift

    def finish(x2, o_att, rw):
        x1 = _combine(o_att, *rw, rw_ln_w[0], rw_ln_b[0], attn_out_norm[0], x2, w_out_bf)
        pq = _norm_matmul(x1, norm2[0], wq_bf, 512)
        e, gate = _peer_route(pq, peer_q_norm[0], peer_subkeys[0])
        return _peer_experts(e, gate, x1, norm2[0], uv)

    def compress_kv(pool_k, pool_v, pt):
        ck = _compress(pool_k, pt, cmp_w1_k[0], cmp_w2_k[0], cmp_pe_k[0], k_norm_cmp[0], True)
        cv = _compress(pool_v, pt, cmp_w1_v[0], cmp_w2_v[0], cmp_pe_v[0], k_norm_cmp[0], False)
        return ck, cv

    kv5 = lambda z, nb, tl: z.reshape(1, nb, tl, KV_HEADS, ATT_HD)

    tm_pre = min(256, b * t)
    x2, (q, gates, kc, vc, ks, vs, kw, vw), rw, wkv_p, shift_p = group(
        x_prompt, jnp.arange(t), t, jnp.zeros((tm_pre, RW_PROJ_PAD), F32),
        jnp.zeros((b, RW_HEADS, RW_HD, RW_HD), F32))
    ident = jnp.arange(b * (t // PAGE), dtype=I32).reshape(b, t // PAGE)
    as_pool = lambda z: z.reshape(1, -1, PAGE, KV_HEADS, ATT_HD)
    ck, cv = compress_kv(as_pool(kc), as_pool(vc), ident)
    gates_h = gates.reshape(b * t, KV_HEADS, 3 * ATT_GROUP).transpose(1, 0, 2)
    o_att = _nsa_prompt(q, gates_h, ck, cv, ks, vs, kw, vw, b, t)
    y_prompt = finish(x2, o_att, rw).reshape(b, t, D_MODEL)
    wp = min(WINDOW, t)
    p_state = (kv5(kc, b, t), kv5(vc, b, t), kv5(ks, b, t), kv5(vs, b, t),
               kv5(kw, b, t)[:, :, t - wp:], kv5(vw, b, t)[:, :, t - wp:], wkv_p[None], shift_p[None])

    prev = jnp.repeat(_pad_rw(state_shift[0]), ts, axis=0)
    x2, (q, gates, kc, vc, ks, vs, kw, vw), rw, wkv_s, shift_s = group(
        x_sample, jnp.tile(n_past + jnp.arange(ts), db), ts, prev, state_wkv[0])
    ck, cv = compress_kv(pools[0], pools[1], page_table)
    tpad = 8
    pad_t = lambda z, n: jnp.pad(z.reshape(db, ts, -1), ((0, 0), (0, n - ts), (0, 0)))
    kwb, vwb = state_k_win[0].reshape(db, wbuf, KV_WIDTH), state_v_win[0].reshape(db, wbuf, KV_WIDTH)
    o_att = _nsa_sample(pad_t(q, tpad), pad_t(gates, tpad), ck, cv, pools[2], pools[3], page_table,
                        pad_t(ks, PAGE), pad_t(vs, PAGE), kwb, vwb, pad_t(kw, PAGE), pad_t(vw, PAGE), ts)
    y_sample = finish(x2, o_att[:, :ts].reshape(db * ts, ATT_WIDTH), rw).reshape(db, ts, D_MODEL)
    kw_new = jnp.concatenate([kwb, kw.reshape(db, ts, KV_WIDTH)], axis=1)[:, -wbuf:]
    vw_new = jnp.concatenate([vwb, vw.reshape(db, ts, KV_WIDTH)], axis=1)[:, -wbuf:]
    s_state = (kv5(kc, db, ts), kv5(vc, db, ts), kv5(ks, db, ts), kv5(vs, db, ts),
               kv5(kw_new, db, wbuf), kv5(vw_new, db, wbuf), wkv_s[None], shift_s[None])
    return (y_prompt, y_sample, *p_state, *s_state)
```

```python
import functools
import math

import jax
import jax.numpy as jnp
from jax import lax
from jax.experimental import pallas as pl
from jax.experimental.pallas import tpu as pltpu

F32, BF16, I32 = jnp.float32, jnp.bfloat16, jnp.int32

D_MODEL = 2048
PAGE = 128
ATT_HEADS, ATT_HD, KV_HEADS, ATT_GROUP = 8, 128, 2, 4
ATT_WIDTH, KV_WIDTH = ATT_HEADS * ATT_HD, KV_HEADS * ATT_HD
RW_HD, RW_WIDTH, RW_HEADS = 64, 1024, 16
ROPE_DIM, ROPE_THETA = 32, 500000.0
CMP_LEN, CMP_STRIDE = 32, 16
SLC_BLOCK, SLC_TOPK, WINDOW, Q_BLOCK = 64, 16, 512, 128
DECAY_LORA, A_LORA, GATE_LORA = 96, 96, 256
LORA_PAD = 128
RW_PROJ = 3 * RW_WIDTH + DECAY_LORA + A_LORA + GATE_LORA
RW_PROJ_PAD = 3 * RW_WIDTH + 2 * LORA_PAD + GATE_LORA
QKV_WIDTH = ATT_WIDTH + 6 * KV_WIDTH
QKVG_PAD = QKV_WIDTH + 128
PEER_HEADS, PEER_DK, PEER_TOPK = 8, 256, 16
PEER_SLOTS = 8
PEER_AHEAD = PEER_SLOTS - 1
NORM_EPS, GN_EPS, NEG_INF, FORCE = 1e-6, 64e-5, -1e30, 1e4
REMOVED = -3e38
VMEM_LIMIT = 56 * 1024 * 1024


def _cp(*sem):
    return pltpu.CompilerParams(dimension_semantics=sem, vmem_limit_bytes=VMEM_LIMIT)


def _dot(a, b):
    return jnp.dot(a.astype(BF16), b.astype(BF16), preferred_element_type=F32)


def _dot_nt(a, b):
    return lax.dot_general(a.astype(BF16), b.astype(BF16), (((1,), (1,)), ((), ())),
                           preferred_element_type=F32)


def _split_dot(x, m_bf16, terms):
    out = None
    for _ in range(terms):
        hi = x.astype(BF16)
        part = jnp.dot(hi, m_bf16, preferred_element_type=F32)
        out = part if out is None else out + part
        x = x - hi.astype(F32)
    return out


def _rms(x, g):
    return x * lax.rsqrt(jnp.mean(x * x, axis=-1, keepdims=True) + NORM_EPS) * g


def _masked_softmax(s, mask):
    s = jnp.where(mask, s, NEG_INF)
    e = jnp.where(mask, jnp.exp(s - jnp.max(s, axis=-1, keepdims=True)), 0.0)
    return e / jnp.maximum(jnp.sum(e, axis=-1, keepdims=True), 1e-30)


def _iota(shape, dim):
    return lax.broadcasted_iota(I32, shape, dim)


def _norm_mm_kernel(x_ref, g_ref, w_ref, o_ref, xn_ref):
    @pl.when(pl.program_id(1) == 0)
    def _():
        xn_ref[...] = _rms(x_ref[...], g_ref[...]).astype(BF16)

    o_ref[...] = jnp.dot(xn_ref[...], w_ref[...], preferred_element_type=F32)


def _norm_matmul(x, g, w_bf16, tn):
    m, k = x.shape
    n = w_bf16.shape[1]
    tm = min(512, m)
    return pl.pallas_call(
        _norm_mm_kernel,
        grid=(m // tm, n // tn),
        in_specs=[pl.BlockSpec((tm, k), lambda i, j: (i, 0)),
                  pl.BlockSpec((1, k), lambda i, j: (0, 0)),
                  pl.BlockSpec((k, tn), lambda i, j: (0, j))],
        out_specs=pl.BlockSpec((tm, tn), lambda i, j: (i, j)),
        out_shape=jax.ShapeDtypeStruct((m, n), F32),
        scratch_shapes=[pltpu.VMEM((tm, k), BF16)],
        compiler_params=_cp("arbitrary", "arbitrary"),
    )(x, g.reshape(1, k), w_bf16)


def _qkv_post_kernel(p_ref, cos_ref, sa_ref, sb_ref, gq_ref, gs_ref, gw_ref,
                     q_ref, kc_ref, vc_ref, ks_ref, vs_ref, kw_ref, vw_ref):
    cos, sa, sb = cos_ref[...], sa_ref[...], sb_ref[...]

    def rope(x):
        return x * cos + pltpu.roll(x, ATT_HD - ROPE_DIM // 2, 1) * sa + pltpu.roll(x, ROPE_DIM // 2, 1) * sb

    def head(c):
        return p_ref[:, c * ATT_HD:(c + 1) * ATT_HD]

    for h in range(ATT_HEADS):
        q_ref[:, h * ATT_HD:(h + 1) * ATT_HD] = rope(_rms(head(h), gq_ref[...]))
    for h in range(KV_HEADS):
        sl = slice(h * ATT_HD, (h + 1) * ATT_HD)
        kc_ref[:, sl] = rope(head(8 + h))
        vc_ref[:, sl] = head(10 + h)
        ks_ref[:, sl] = rope(_rms(head(12 + h), gs_ref[...]))
        vs_ref[:, sl] = head(14 + h)
        kw_ref[:, sl] = rope(_rms(head(16 + h), gw_ref[...]))
        vw_ref[:, sl] = head(18 + h)


def _rope_tables(pos):
    half = ROPE_DIM // 2
    inv = ROPE_THETA ** (-jnp.arange(0, ROPE_DIM, 2, dtype=F32) / ROPE_DIM)
    ang = pos.astype(F32)[:, None] * inv[None, :]
    cos, sin = jnp.cos(ang), jnp.sin(ang)
    n = pos.shape[0]
    rest = ATT_HD - ROPE_DIM
    cos_t = jnp.concatenate([cos, cos, jnp.ones((n, rest), F32)], axis=1)
    sin_a = jnp.concatenate([-sin, jnp.zeros((n, half + rest), F32)], axis=1)
    sin_b = jnp.concatenate([jnp.zeros((n, half), F32), sin, jnp.zeros((n, rest), F32)], axis=1)
    return cos_t, sin_a, sin_b


def _qkv_post(proj, pos, q_norm, k_norm_slc, k_norm_win):
    m = proj.shape[0]
    p = pos.shape[0]
    tm = min(256, m, p)
    reps = p // tm
    tabs = _rope_tables(pos)
    row = lambda w: pl.BlockSpec((tm, w), lambda i: (i, 0))
    tab = pl.BlockSpec((tm, ATT_HD), lambda i: (i % reps, 0))
    gain = pl.BlockSpec((1, ATT_HD), lambda i: (0, 0))
    kv = jax.ShapeDtypeStruct((m, KV_WIDTH), F32)
    return pl.pallas_call(
        _qkv_post_kernel,
        grid=(m // tm,),
        in_specs=[row(QKV_WIDTH), tab, tab, tab, gain, gain, gain],
        out_specs=[row(ATT_WIDTH)] + [row(KV_WIDTH)] * 6,
        out_shape=[jax.ShapeDtypeStruct((m, ATT_WIDTH), F32)] + [kv] * 6,
        compiler_params=_cp("arbitrary"),
    )(proj, *tabs, q_norm.reshape(1, -1), k_norm_slc.reshape(1, -1), k_norm_win.reshape(1, -1))


def _cmp_proj_kernel(pt_ref, *refs, n_in):
    x_refs, w_ref, o_ref, xc_ref = refs[:n_in], refs[n_in], refs[n_in + 1], refs[n_in + 2]
    cpp = PAGE // CMP_STRIDE
    half = n_in * cpp
    for g in range(n_in):
        for h in range(KV_HEADS):
            for c in range(CMP_STRIDE):
                xc_ref[h * half + g * cpp:h * half + (g + 1) * cpp, c * ATT_HD:(c + 1) * ATT_HD] = (
                    x_refs[g][pl.ds(c, cpp, stride=CMP_STRIDE), h, :])
    pp = _dot(xc_ref[...], w_ref[...])
    o_ref[0] = pp[:half]
    o_ref[1] = pp[half:]


def _cmp_project(pool, page_table, w1):
    nb, n_pages = page_table.shape
    n_in = math.gcd(32, n_pages)
    cpp = PAGE // CMP_STRIDE
    w1r = w1.reshape(2, CMP_STRIDE * ATT_HD, ATT_HD)
    w1cat = jnp.concatenate([w1r[0], w1r[1]], axis=1).astype(BF16)

    def xspec(g):
        return pl.BlockSpec((None, None, PAGE, KV_HEADS, ATT_HD),
                            lambda b, j, pt: (0, pt[b * n_pages + j * n_in + g], 0, 0, 0))

    return pl.pallas_call(
        functools.partial(_cmp_proj_kernel, n_in=n_in),
        grid_spec=pltpu.PrefetchScalarGridSpec(
            num_scalar_prefetch=1,
            grid=(nb, n_pages // n_in),
            in_specs=[xspec(g) for g in range(n_in)]
            + [pl.BlockSpec((CMP_STRIDE * ATT_HD, 2 * ATT_HD), lambda b, j, pt: (0, 0))],
            out_specs=pl.BlockSpec((None, KV_HEADS, n_in * cpp, 2 * ATT_HD), lambda b, j, pt: (b, 0, j, 0)),
            scratch_shapes=[pltpu.VMEM((KV_HEADS * n_in * cpp, CMP_STRIDE * ATT_HD), F32)]),
        out_shape=jax.ShapeDtypeStruct((nb, KV_HEADS, n_pages * cpp, 2 * ATT_HD), F32),
        compiler_params=_cp("arbitrary", "arbitrary"),
    )(page_table.reshape(-1), *([pool] * n_in), w1cat)


def _cmp_mlp_kernel(pp_ref, pe_ref, w1_ref, w2_ref, gn_ref, o_ref, *, use_norm):
    pp = pp_ref[...]
    n = pp.shape[0]
    bias = _dot(pe_ref[...], w1_ref[...])[0:1]
    hid = pp[:, :ATT_HD] + pltpu.roll(pp[:, ATT_HD:], n - 1, 0) + bias
    out = _dot(jax.nn.gelu(hid), w2_ref[...])
    if use_norm:
        out = _rms(out, gn_ref[...])
    o_ref[...] = jnp.where(_iota((n, 1), 0) < n - 1, out, 0.0)


def _cmp_mlp(pp, pe, w1, w2, gn, use_norm):
    nb, _, n_chunk, _ = pp.shape
    flat = CMP_LEN * ATT_HD
    pe8 = jnp.broadcast_to(pe.reshape(1, flat), (8, flat))
    return pl.pallas_call(
        functools.partial(_cmp_mlp_kernel, use_norm=use_norm),
        grid=(nb, KV_HEADS),
        in_specs=[pl.BlockSpec((None, None, n_chunk, 2 * ATT_HD), lambda b, h: (b, h, 0, 0)),
                  pl.BlockSpec((8, flat), lambda b, h: (0, 0)),
                  pl.BlockSpec((flat, ATT_HD), lambda b, h: (0, 0)),
                  pl.BlockSpec((ATT_HD, ATT_HD), lambda b, h: (0, 0)),
                  pl.BlockSpec((1, ATT_HD), lambda b, h: (0, 0))],
        out_specs=pl.BlockSpec((None, None, n_chunk, ATT_HD), lambda b, h: (b, h, 0, 0)),
        out_shape=jax.ShapeDtypeStruct((nb, KV_HEADS, n_chunk, ATT_HD), F32),
        compiler_params=_cp("arbitrary", "arbitrary"),
    )(pp, pe8, w1.reshape(flat, ATT_HD).astype(BF16), w2.astype(BF16), gn.reshape(1, -1))


def _compress(pool, page_table, w1, w2, pe, gn, use_norm):
    return _cmp_mlp(_cmp_project(pool, page_table, w1), pe, w1, w2, gn, use_norm)


def _overlap(n_rows, n_cols, n_cmp):
    ci, sj = _iota((n_rows, n_cols), 0), _iota((n_rows, n_cols), 1)
    ov = ((ci * CMP_STRIDE < (sj + 1) * SLC_BLOCK) & (ci * CMP_STRIDE + CMP_LEN > sj * SLC_BLOCK)
          & (ci < n_cmp))
    return ov.astype(BF16)


def _select_blocks(imp, qpos, n_slc, top_n):
    blk = _iota(imp.shape, 1)
    cur = qpos // SLC_BLOCK
    forced = (blk == 0) | (blk == cur) | (blk == cur - 1)
    val = jnp.where(forced, FORCE, jnp.where(blk <= cur, imp, -FORCE))
    val = jnp.where(blk < n_slc, val, REMOVED)
    member = jnp.zeros(imp.shape, F32)
    for _ in range(top_n):
        m = jnp.max(val, axis=-1, keepdims=True)
        idx = jnp.min(jnp.where(val == m, blk, imp.shape[1]), axis=-1, keepdims=True)
        hit = blk == idx
        member = jnp.where(hit, 1.0, member)
        val = jnp.where(hit, REMOVED, val)
    return member


def _expand_blocks(member_bf16, first_key, n_keys):
    lanes = member_bf16.shape[1]
    jj, cc = _iota((lanes, n_keys), 0), _iota((lanes, n_keys), 1)
    sel = (jj == (first_key + cc) // SLC_BLOCK).astype(BF16)
    return jnp.dot(member_bf16, sel, preferred_element_type=F32)


def _online_update(carry, s, valid, v_bf16):
    m, l, acc = carry
    s = jnp.where(valid, s, NEG_INF)
    m_new = jnp.maximum(m, jnp.max(s, axis=-1, keepdims=True))
    p = jnp.where(valid, jnp.exp(s - m_new), 0.0)
    alpha = jnp.exp(m - m_new)
    l = l * alpha + jnp.sum(p, axis=-1, keepdims=True)
    acc = acc * alpha + jnp.dot(p.astype(BF16), v_bf16, preferred_element_type=F32)
    return m_new, l, acc


def _nsa_prompt_kernel(q_ref, gt_ref, ck_ref, cv_ref, ks_ref, vs_ref, kw_ref, vw_ref, o_ref,
                       *, n_cmp, n_slc, lanes, top_n, kt, band):
    i = pl.program_id(2)
    s0 = i * Q_BLOCK
    scale = ATT_HD ** -0.5
    rows = ATT_GROUP * Q_BLOCK
    qs = jnp.concatenate([q_ref[:, g * ATT_HD:(g + 1) * ATT_HD] for g in range(ATT_GROUP)],
                         axis=0).astype(BF16)
    qpos4 = s0 + (_iota((rows, 1), 0) & (Q_BLOCK - 1))
    qpos = s0 + _iota((Q_BLOCK, 1), 0)

    nc = ck_ref.shape[0]
    sc = _dot_nt(qs, ck_ref[...]) * scale
    cidx = _iota((1, nc), 1)
    p_cmp = _masked_softmax(sc, (cidx * CMP_STRIDE + CMP_LEN - 1 <= qpos4) & (cidx < n_cmp))
    o_cmp = _dot(p_cmp, cv_ref[...])
    p_sum = sum(p_cmp[g * Q_BLOCK:(g + 1) * Q_BLOCK] for g in range(ATT_GROUP))
    imp = _split_dot(p_sum, _overlap(nc, lanes, n_cmp), 2)
    member = _select_blocks(imp, qpos, n_slc, top_n).astype(BF16)
    member4 = jnp.concatenate([member] * ATT_GROUP, axis=0)

    def body(t, carry):
        k0 = pl.multiple_of(t * kt, kt)
        s = _dot_nt(qs, ks_ref[pl.ds(k0, kt), :]) * scale
        kpos = k0 + _iota((1, kt), 1)
        valid = (_expand_blocks(member4, k0, kt) > 0.5) & (kpos <= qpos4)
        return _online_update(carry, s, valid, vs_ref[pl.ds(k0, kt), :].astype(BF16))

    init = (jnp.full((rows, 1), NEG_INF, F32), jnp.zeros((rows, 1), F32), jnp.zeros((rows, ATT_HD), F32))
    _, l, acc = lax.fori_loop(0, (s0 + Q_BLOCK + kt - 1) // kt, body, init)
    o_slc = acc / jnp.maximum(l, 1e-30)

    start = pl.multiple_of(jnp.maximum(s0 + Q_BLOCK - band, 0), Q_BLOCK)
    wpos = start + _iota((1, band), 1)
    sw = _dot_nt(qs, kw_ref[pl.ds(start, band), :]) * scale
    p_win = _masked_softmax(sw, (wpos <= qpos4) & (wpos > qpos4 - WINDOW))
    o_win = _dot(p_win, vw_ref[pl.ds(start, band), :])

    gate = jax.nn.sigmoid(gt_ref[...])
    for g in range(ATT_GROUP):
        r = slice(g * Q_BLOCK, (g + 1) * Q_BLOCK)
        o_ref[:, g * ATT_HD:(g + 1) * ATT_HD] = (gate[:, 3 * g:3 * g + 1] * o_cmp[r]
                                                 + gate[:, 3 * g + 1:3 * g + 2] * o_slc[r]
                                                 + gate[:, 3 * g + 2:3 * g + 3] * o_win[r])


def _nsa_prompt(q, gates, ck, cv, ks, vs, kw, vw, b, t):
    nq = t // Q_BLOCK
    n_chunk = ck.shape[2]
    n_slc = -(-t // SLC_BLOCK)
    lanes = -(-n_slc // 128) * 128
    kt = 256 if t % 256 == 0 else Q_BLOCK
    band = min(WINDOW + Q_BLOCK, t)
    qspec = pl.BlockSpec((Q_BLOCK, ATT_GROUP * ATT_HD), lambda bb, h, i: (bb * nq + i, h))
    cspec = pl.BlockSpec((None, None, n_chunk, ATT_HD), lambda bb, h, i: (bb, h, 0, 0))
    kspec = pl.BlockSpec((t, ATT_HD), lambda bb, h, i: (bb, h))
    return pl.pallas_call(
        functools.partial(_nsa_prompt_kernel, n_cmp=n_chunk - 1, n_slc=n_slc, lanes=lanes,
                          top_n=min(SLC_TOPK, n_slc), kt=kt, band=band),
        grid=(b, KV_HEADS, nq),
        in_specs=[qspec, pl.BlockSpec((None, Q_BLOCK, 3 * ATT_GROUP), lambda bb, h, i: (h, bb * nq + i, 0)),
                  cspec, cspec, kspec, kspec, kspec, kspec],
        out_specs=qspec,
        out_shape=jax.ShapeDtypeStruct((b * t, ATT_WIDTH), F32),
        compiler_params=_cp("arbitrary", "arbitrary", "arbitrary"),
    )(q, gates, ck, cv, ks, vs, kw, vw)


def _nsa_sample_kernel(pt_ref, q_ref, gt_ref, ck_ref, cv_ref, *refs,
                       n_in, ts, tpad, n_past, n_cmp, n_slc, top_n, wbuf):
    kpages, vpages = refs[:n_in], refs[n_in:2 * n_in]
    (ksn_ref, vsn_ref, kwb_ref, vwb_ref, kwn_ref, vwn_ref, o_ref,
     mem_ref, ocmp_ref, m_ref, l_ref, acc_ref) = refs[2 * n_in:]
    j = pl.program_id(1)
    scale = ATT_HD ** -0.5
    rows = ATT_GROUP * tpad
    tok = _iota((rows, 1), 0) & (tpad - 1)
    qpos = n_past + tok
    nkeys = n_in * PAGE

    def q_of(h):
        return jnp.concatenate([q_ref[:, (h * ATT_GROUP + g) * ATT_HD:(h * ATT_GROUP + g + 1) * ATT_HD]
                                for g in range(ATT_GROUP)], axis=0).astype(BF16)

    def hcols(ref, h):
        return ref[:, h * ATT_HD:(h + 1) * ATT_HD]

    @pl.when(j == 0)
    def _():
        nc = ck_ref.shape[1]
        lanes = mem_ref.shape[2]
        cidx = _iota((1, nc), 1)
        for h in range(KV_HEADS):
            sc = _dot_nt(q_of(h), ck_ref[h]) * scale
            p_cmp = _masked_softmax(sc, (cidx * CMP_STRIDE + CMP_LEN - 1 <= qpos) & (cidx < n_cmp))
            ocmp_ref[h] = _dot(p_cmp, cv_ref[h])
            p_sum = sum(p_cmp[g * tpad:(g + 1) * tpad] for g in range(ATT_GROUP))
            imp = _split_dot(p_sum, _overlap(nc, lanes, n_cmp), 2)
            member = _select_blocks(imp, qpos[:tpad], n_slc, top_n)
            mem_ref[h] = jnp.concatenate([member] * ATT_GROUP, axis=0)
            m_ref[h] = jnp.full((rows, 1), NEG_INF, F32)
            l_ref[h] = jnp.zeros((rows, 1), F32)
            acc_ref[h] = jnp.zeros((rows, ATT_HD), F32)

    k0 = j * nkeys
    kpos = k0 + _iota((1, nkeys), 1)
    for h in range(KV_HEADS):
        kcat = jnp.concatenate([r[:, h, :] for r in kpages], axis=0).astype(BF16)
        vcat = jnp.concatenate([r[:, h, :] for r in vpages], axis=0).astype(BF16)
        s = _dot_nt(q_of(h), kcat) * scale
        valid = (_expand_blocks(mem_ref[h].astype(BF16), k0, nkeys) > 0.5) & (kpos <= qpos)
        m, l, acc = _online_update((m_ref[h], l_ref[h], acc_ref[h]), s, valid, vcat)
        m_ref[h], l_ref[h], acc_ref[h] = m, l, acc

    @pl.when(j == pl.num_programs(1) - 1)
    def _():
        gate = jax.nn.sigmoid(gt_ref[...])
        npad = ksn_ref.shape[0]
        ridx = _iota((1, npad), 1)
        new_ok = (ridx < ts) & (n_past + ridx <= qpos)
        cidx = _iota((1, wbuf + npad), 1)
        wpos = jnp.where(cidx < wbuf, n_past - wbuf + cidx, n_past + cidx - wbuf)
        w_ok = ((cidx - wbuf < ts) & (wpos <= qpos) & (wpos > qpos - WINDOW) & (wpos >= 0))
        for h in range(KV_HEADS):
            qh = q_of(h)
            blk = n_past // SLC_BLOCK
            s = _dot_nt(qh, hcols(ksn_ref, h)) * scale
            valid = (mem_ref[h][:, blk:blk + 1] > 0.5) & new_ok
            _, l, acc = _online_update((m_ref[h], l_ref[h], acc_ref[h]), s, valid,
                                       hcols(vsn_ref, h).astype(BF16))
            o_slc = acc / jnp.maximum(l, 1e-30)
            kw = jnp.concatenate([hcols(kwb_ref, h), hcols(kwn_ref, h)], axis=0)
            vw = jnp.concatenate([hcols(vwb_ref, h), hcols(vwn_ref, h)], axis=0)
            p_win = _masked_softmax(_dot_nt(qh, kw) * scale, w_ok)
            o_win = _dot(p_win, vw)
            o_cmp = ocmp_ref[h]
            for g in range(ATT_GROUP):
                r = slice(g * tpad, (g + 1) * tpad)
                c = 3 * (h * ATT_GROUP + g)
                hh = h * ATT_GROUP + g
                o_ref[:, hh * ATT_HD:(hh + 1) * ATT_HD] = (gate[:, c:c + 1] * o_cmp[r]
                                                           + gate[:, c + 1:c + 2] * o_slc[r]
                                                           + gate[:, c + 2:c + 3] * o_win[r])


def _nsa_sample(q, gates, ck, cv, pool_k, pool_v, page_table, ks_new, vs_new, kw_buf, vw_buf,
                kw_new, vw_new, ts):
    db, tpad, _ = q.shape
    n_pages = page_table.shape[1]
    n_past = n_pages * PAGE
    n_in = math.gcd(8, n_pages)
    n_chunk = ck.shape[2]
    n_slc = -(-(n_past + ts) // SLC_BLOCK)
    lanes = -(-n_slc // 128) * 128
    wbuf = kw_buf.shape[1]
    npad = ks_new.shape[1]
    rows = ATT_GROUP * tpad
    per_b = lambda s1, s2: pl.BlockSpec((None, s1, s2), lambda b, j, pt: (b, 0, 0))

    def pspec(g):
        return pl.BlockSpec((None, None, PAGE, KV_HEADS, ATT_HD),
                            lambda b, j, pt: (0, pt[b * n_pages + j * n_in + g], 0, 0, 0))

    cspec = pl.BlockSpec((None, KV_HEADS, n_chunk, ATT_HD), lambda b, j, pt: (b, 0, 0, 0))
    return pl.pallas_call(
        functools.partial(_nsa_sample_kernel, n_in=n_in, ts=ts, tpad=tpad, n_past=n_past,
                          n_cmp=n_chunk - 1, n_slc=n_slc, top_n=min(SLC_TOPK, n_slc), wbuf=wbuf),
        grid_spec=pltpu.PrefetchScalarGridSpec(
            num_scalar_prefetch=1,
            grid=(db, n_pages // n_in),
            in_specs=[per_b(tpad, ATT_WIDTH), per_b(tpad, 3 * ATT_HEADS), cspec, cspec]
            + [pspec(g) for g in range(n_in)] * 2
            + [per_b(npad, KV_WIDTH)] * 2 + [per_b(wbuf, KV_WIDTH)] * 2 + [per_b(npad, KV_WIDTH)] * 2,
            out_specs=per_b(tpad, ATT_WIDTH),
            scratch_shapes=[pltpu.VMEM((KV_HEADS, rows, lanes), F32),
                            pltpu.VMEM((KV_HEADS, rows, ATT_HD), F32),
                            pltpu.VMEM((KV_HEADS, rows, 1), F32),
                            pltpu.VMEM((KV_HEADS, rows, 1), F32),
                            pltpu.VMEM((KV_HEADS, rows, ATT_HD), F32)]),
        out_shape=jax.ShapeDtypeStruct((db, tpad, ATT_WIDTH), F32),
        compiler_params=_cp("arbitrary", "arbitrary"),
    )(page_table.reshape(-1), q, gates, ck, cv, *([pool_k] * n_in), *([pool_v] * n_in),
      ks_new, vs_new, kw_buf, vw_buf, kw_new, vw_new)


def _seg_mats():
    seg = (_iota((RW_WIDTH, 128), 0) // RW_HD == _iota((RW_WIDTH, 128), 1)).astype(BF16)
    seg_t = (_iota((128, RW_WIDTH), 1) // RW_HD == _iota((128, RW_WIDTH), 0)).astype(BF16)
    return seg, seg_t


def _rwkv_pre_kernel(f_ref, pf_ref, mu_ref, w0_ref, w2_ref, a0_ref, a2_ref, g2_ref, kk_w_ref, ka_w_ref,
                     rk_ref, r_ref, d_ref, k_ref, v_ref, kk_ref, kka_ref, g_ref, bonus_ref, carry_ref,
                     *, seq_len):
    tm = f_ref.shape[0]

    @pl.when(pl.program_id(0) == 0)
    def _():
        carry_ref[...] = jnp.zeros_like(carry_ref)

    feat = f_ref[...]
    row = _iota((tm, 1), 0)
    shifted = jnp.where(row == 0, carry_ref[...], pltpu.roll(feat, 1, 0))
    shifted = jnp.where((pl.program_id(0) * tm + row) % seq_len == 0, pf_ref[...], shifted)
    carry_ref[...] = feat[tm - 1:tm]
    mixed = feat + (shifted - feat) * mu_ref[...]
    w = RW_WIDTH
    r, k, v = mixed[:, :w], mixed[:, w:2 * w], mixed[:, 2 * w:3 * w]
    wd = mixed[:, 3 * w:3 * w + LORA_PAD]
    ad = mixed[:, 3 * w + LORA_PAD:3 * w + 2 * LORA_PAD]
    gd = mixed[:, 3 * w + 2 * LORA_PAD:]
    wlog = -jax.nn.softplus(-(w0_ref[...] + _dot(jnp.tanh(wd), w2_ref[...]))) - 0.5
    decay = jnp.exp(-jnp.exp(wlog))
    a = jax.nn.sigmoid(a0_ref[...] + _dot(ad, a2_ref[...]))
    g = _dot(jax.nn.sigmoid(gd), g2_ref[...])
    seg, seg_t = _seg_mats()
    kk = k * kk_w_ref[...]
    norm = jnp.maximum(jnp.sqrt(_split_dot(kk * kk, seg, 2)), 1e-12)
    kk = kk * _split_dot(1.0 / norm, seg_t, 2)
    k2 = k * (1.0 + (a - 1.0) * ka_w_ref[...])
    bonus = _split_dot(_split_dot(r * k2 * rk_ref[...], seg, 2), seg_t, 2) * v
    r_ref[...], d_ref[...], k_ref[...], v_ref[...] = r, decay, k2, v
    kk_ref[...], kka_ref[...], g_ref[...], bonus_ref[...] = kk, kk * a, g, bonus


def _rwkv_pre(feat, prev, seq_len, mu, w0, w2, a0, a2, g2, k_k, k_a, r_k):
    m = feat.shape[0]
    tm = min(256, m)
    vec = lambda n: pl.BlockSpec((1, n), lambda i: (0, 0))
    mat = lambda a, b: pl.BlockSpec((a, b), lambda i: (0, 0))
    row = pl.BlockSpec((tm, RW_WIDTH), lambda i: (i, 0))
    prev_blocks = prev.shape[0] // tm
    return pl.pallas_call(
        functools.partial(_rwkv_pre_kernel, seq_len=seq_len),
        grid=(m // tm,),
        in_specs=[pl.BlockSpec((tm, RW_PROJ_PAD), lambda i: (i, 0)),
                  pl.BlockSpec((tm, RW_PROJ_PAD), lambda i: (i % prev_blocks, 0)),
                  vec(RW_PROJ_PAD), vec(RW_WIDTH), mat(LORA_PAD, RW_WIDTH), vec(RW_WIDTH),
                  mat(LORA_PAD, RW_WIDTH), mat(GATE_LORA, RW_WIDTH), vec(RW_WIDTH), vec(RW_WIDTH),
                  vec(RW_WIDTH)],
        out_specs=[row] * 8,
        out_shape=[jax.ShapeDtypeStruct((m, RW_WIDTH), F32)] * 8,
        scratch_shapes=[pltpu.VMEM((1, RW_PROJ_PAD), F32)],
        compiler_params=_cp("arbitrary"),
    )(feat, prev, mu.reshape(1, -1), w0.reshape(1, -1), w2.astype(BF16), a0.reshape(1, -1),
      a2.astype(BF16), g2.astype(BF16), k_k.reshape(1, -1), k_a.reshape(1, -1), r_k.reshape(1, -1))


def _wkv_scan_kernel(r_ref, d_ref, k_ref, v_ref, kk_ref, kka_ref, s0_ref, o_ref, sf_ref, st_ref):
    c = pl.program_id(1)
    bg, steps = r_ref.shape[0], r_ref.shape[1]
    chains = [(b, p) for b in range(bg) for p in range(RW_HEADS // 2)]

    @pl.when(c == 0)
    def _():
        st_ref[...] = s0_ref[...]

    diag = ((_iota((RW_HD, 128), 1) & (RW_HD - 1)) == _iota((RW_HD, 128), 0)).astype(F32)
    same_head = ((_iota((128, 128), 0) // RW_HD) == (_iota((128, 128), 1) // RW_HD)).astype(BF16)
    same_head2 = jnp.concatenate([same_head, same_head], axis=0)

    def split(x):
        hi = x.astype(BF16)
        return hi, x - hi.astype(F32)

    def blk(z, i):
        return z[i * RW_HD:(i + 1) * RW_HD]

    def run(base, n):
        tiles = [[ref[b, pl.ds(base, n), :] for b in range(bg)]
                 for ref in (r_ref, d_ref, k_ref, v_ref, kk_ref, kka_ref)]
        s = [st_ref[b, p] for b, p in chains]
        outs = [[] for _ in chains]
        for j in range(n):
            r_, d_, k_, v_, kk_, kka_ = ([t[b][j:j + 1, p * 128:(p + 1) * 128] for b, p in chains]
                                         for t in tiles)
            lhs = []
            for i in range(len(chains)):
                hi, lo = split(s[i] * kk_[i])
                lhs.append(jnp.concatenate([hi, lo.astype(BF16)], axis=1))
            s_kk = jnp.dot(jnp.concatenate(lhs, axis=0), same_head2, preferred_element_type=F32)
            lhs = []
            for i in range(len(chains)):
                hi, lo = split(v_[i])
                lhs.append(jnp.concatenate(
                    [(jnp.broadcast_to(hi.astype(F32), (RW_HD, 128)) * diag).astype(BF16),
                     (jnp.broadcast_to(lo, (RW_HD, 128)) * diag).astype(BF16)], axis=1))
            v_col = jnp.dot(jnp.concatenate(lhs, axis=0), same_head2, preferred_element_type=F32)
            s = [s[i] * d_[i] - blk(s_kk, i) * kka_[i] + blk(v_col, i) * k_[i] for i in range(len(chains))]
            o_sum = jnp.dot(jnp.concatenate([(s[i] * r_[i]).astype(BF16) for i in range(len(chains))], axis=0),
                            same_head, preferred_element_type=F32)
            for i in range(len(chains)):
                outs[i].append(jnp.sum(blk(o_sum, i) * diag, axis=0, keepdims=True))
        for i, (b, p) in enumerate(chains):
            st_ref[b, p] = s[i]
            o_ref[b, pl.ds(base, n), p * 128:(p + 1) * 128] = jnp.concatenate(outs[i], axis=0)

    if steps % 8 == 0:
        def group(i, carry):
            run(pl.multiple_of(i * 8, 8), 8)
            return carry

        lax.fori_loop(0, steps // 8, group, 0)
    else:
        run(0, steps)

    @pl.when(c == pl.num_programs(1) - 1)
    def _():
        sf_ref[...] = st_ref[...]


def _wkv_scan(r, d, k, v, kk, kka, s0):
    b, t, _ = r.shape
    tc = math.gcd(64, t)
    bg = math.gcd(2, b)
    n_pairs = RW_HEADS // 2
    seq = pl.BlockSpec((bg, tc, RW_WIDTH), lambda bb, c: (bb, c, 0))
    st = pl.BlockSpec((bg, n_pairs, RW_HD, 128), lambda bb, c: (bb, 0, 0, 0))
    return pl.pallas_call(
        _wkv_scan_kernel,
        grid=(b // bg, t // tc),
        in_specs=[seq] * 6 + [st],
        out_specs=[seq, st],
        out_shape=[jax.ShapeDtypeStruct((b, t, RW_WIDTH), F32),
                   jax.ShapeDtypeStruct((b, n_pairs, RW_HD, 128), F32)],
        scratch_shapes=[pltpu.VMEM((bg, n_pairs, RW_HD, 128), F32)],
        compiler_params=_cp("arbitrary", "arbitrary"),
    )(r, d, k, v, kk, kka, s0)


def _pack_state(s):
    b = s.shape[0]
    return s.reshape(b, RW_HEADS // 2, 2, RW_HD, RW_HD).transpose(0, 1, 3, 2, 4).reshape(
        b, RW_HEADS // 2, RW_HD, 2 * RW_HD)


def _unpack_state(s):
    b = s.shape[0]
    return s.reshape(b, RW_HEADS // 2, RW_HD, 2, RW_HD).transpose(0, 1, 3, 2, 4).reshape(
        b, RW_HEADS, RW_HD, RW_HD)


def _combine_kernel(oa_ref, ow_ref, bonus_ref, g_ref, lnw_ref, lnb_ref, an_ref, x_ref, w_ref, o_ref, cat_ref):
    @pl.when(pl.program_id(1) == 0)
    def _():
        for h in range(ATT_HEADS):
            sl = slice(h * ATT_HD, (h + 1) * ATT_HD)
            cat_ref[:, sl] = _rms(oa_ref[:, sl], an_ref[:, sl]).astype(BF16)
        seg, seg_t = _seg_mats()
        o = ow_ref[...]
        mean = _split_dot(_split_dot(o, seg, 2) * (1.0 / RW_HD), seg_t, 2)
        oc = o - mean
        var = _split_dot(oc * oc, seg, 2) * (1.0 / RW_HD)
        y = oc * _split_dot(lax.rsqrt(var + GN_EPS), seg_t, 2) * lnw_ref[...] + lnb_ref[...]
        cat_ref[:, ATT_WIDTH:] = ((y + bonus_ref[...]) * g_ref[...]).astype(BF16)

    o_ref[...] = x_ref[...] + jnp.dot(cat_ref[...], w_ref[...], preferred_element_type=F32)


def _combine(o_att, o_wkv, bonus, g, ln_w, ln_b, attn_norm, x, w_out_bf16):
    m = x.shape[0]
    tm = min(512, m)
    tn = 512
    rowa = pl.BlockSpec((tm, ATT_WIDTH), lambda i, j: (i, 0))
    roww = pl.BlockSpec((tm, RW_WIDTH), lambda i, j: (i, 0))
    vec = lambda n: pl.BlockSpec((1, n), lambda i, j: (0, 0))
    return pl.pallas_call(
        _combine_kernel,
        grid=(m // tm, D_MODEL // tn),
        in_specs=[rowa, roww, roww, roww, vec(RW_WIDTH), vec(RW_WIDTH), vec(ATT_WIDTH),
                  pl.BlockSpec((tm, tn), lambda i, j: (i, j)),
                  pl.BlockSpec((ATT_WIDTH + RW_WIDTH, tn), lambda i, j: (0, j))],
        out_specs=pl.BlockSpec((tm, tn), lambda i, j: (i, j)),
        out_shape=jax.ShapeDtypeStruct((m, D_MODEL), F32),
        scratch_shapes=[pltpu.VMEM((tm, ATT_WIDTH + RW_WIDTH), BF16)],
        compiler_params=_cp("arbitrary", "arbitrary"),
    )(o_att, o_wkv, bonus, g, ln_w.reshape(1, -1), ln_b.reshape(1, -1), attn_norm.reshape(1, -1), x,
      w_out_bf16)


def _peer_route_kernel(pq_ref, gq_ref, sk1_ref, sk2_ref, e_ref, gate_ref, *, n_keys):
    tm = pq_ref.shape[0]
    half = PEER_DK // 2
    k = PEER_TOPK
    row = _iota((128, tm), 0)
    krow = _iota((n_keys, tm), 0)
    crow = _iota((k * k, tm), 0)
    rep_a = (_iota((k * k, 128), 0) // k == _iota((k * k, 128), 1)).astype(BF16)
    rep_b = (_iota((k * k, 128), 0) % k == _iota((k * k, 128), 1)).astype(BF16)
    e_all = jnp.zeros((128, tm), F32)
    gate_all = jnp.zeros((128, tm), F32)

    def spread(rep, x, terms):
        out = None
        for _ in range(terms):
            hi = x.astype(BF16)
            part = jnp.dot(rep, hi, preferred_element_type=F32)
            out = part if out is None else out + part
            x = x - hi.astype(F32)
        return out

    def top_keys(s):
        val = jnp.zeros((128, tm), F32)
        idx = jnp.zeros((128, tm), F32)
        for j in range(k):
            m = jnp.max(s, axis=0, keepdims=True)
            pick = jnp.min(jnp.where(s == m, krow, n_keys), axis=0, keepdims=True)
            val = jnp.where(row == j, m, val)
            idx = jnp.where(row == j, pick.astype(F32), idx)
            s = jnp.where(krow == pick, REMOVED, s)
        return val, idx

    for h in range(PEER_HEADS):
        q = _rms(pq_ref[:, h * PEER_DK:(h + 1) * PEER_DK], gq_ref[...])
        v1, i1 = top_keys(_dot_nt(sk1_ref[...], q[:, :half]))
        v2, i2 = top_keys(_dot_nt(sk2_ref[...], q[:, half:]))
        cand = spread(rep_a, v1, 3) + spread(rep_b, v2, 3)
        ecand = spread(rep_a, i1, 1) * n_keys + spread(rep_b, i2, 1)
        best = jnp.full((128, tm), REMOVED, F32)
        for j in range(k):
            m = jnp.max(cand, axis=0, keepdims=True)
            pick = jnp.min(jnp.where(cand == m, crow, k * k), axis=0, keepdims=True)
            hit = crow == pick
            e_j = jnp.max(jnp.where(hit, ecand, -1.0), axis=0, keepdims=True)
            e_all = jnp.where(row == h * k + j, e_j, e_all)
            best = jnp.where(row == h * k + j, m, best)
            cand = jnp.where(hit, REMOVED, cand)
        mine = (row >= h * k) & (row < (h + 1) * k)
        ex = jnp.where(mine, jnp.exp(best - jnp.max(best, axis=0, keepdims=True)), 0.0)
        gate_all = gate_all + ex / jnp.sum(ex, axis=0, keepdims=True)

    e_ref[...] = e_all.astype(I32)
    gate_ref[...] = gate_all


def _peer_route(pq, q_norm, subkeys):
    m = pq.shape[0]
    tm = min(256, m)
    n_keys = subkeys.shape[1]
    half = PEER_DK // 2
    out = pl.BlockSpec((128, tm), lambda i: (0, i))
    return pl.pallas_call(
        functools.partial(_peer_route_kernel, n_keys=n_keys),
        grid=(m // tm,),
        in_specs=[pl.BlockSpec((tm, PEER_HEADS * PEER_DK), lambda i: (i, 0)),
                  pl.BlockSpec((1, PEER_DK), lambda i: (0, 0)),
                  pl.BlockSpec((n_keys, half), lambda i: (0, 0)),
                  pl.BlockSpec((n_keys, half), lambda i: (0, 0))],
        out_specs=[out, out],
        out_shape=[jax.ShapeDtypeStruct((128, m), I32), jax.ShapeDtypeStruct((128, m), F32)],
        compiler_params=_cp("arbitrary"),
    )(pq, q_norm.reshape(1, -1), subkeys[0], subkeys[1])


def _peer_expert_kernel(e_ref, gate_ref, x_ref, n2_ref, uv_hbm, o_ref, buf, sem, hb_ref):
    tt = x_ref.shape[0]
    n_sel = PEER_HEADS * PEER_TOPK
    x = x_ref[...]
    hb_ref[...] = _rms(x, n2_ref[...])
    tok_lane = _iota((n_sel, tt), 1)

    def row_copy(idx, slot, j):
        return pltpu.make_async_copy(uv_hbm.at[pl.ds(idx, 1)], buf.at[slot, pl.ds(j, 1)], sem.at[slot])

    def issue(t, slot):
        for j in range(n_sel):
            row_copy(e_ref[j, t], slot, j).start()

    def wait_all(slot):
        pltpu.make_async_copy(uv_hbm.at[pl.ds(0, n_sel)], buf.at[slot], sem.at[slot]).wait()

    for t0 in range(min(PEER_AHEAD, tt)):
        issue(t0, t0)

    def body(t, carry):
        slot = t % PEER_SLOTS

        @pl.when(t + PEER_AHEAD < tt)
        def _():
            issue(t + PEER_AHEAD, (t + PEER_AHEAD) % PEER_SLOTS)

        wait_all(slot)
        h_row = hb_ref[pl.ds(t, 1), :]
        act = jnp.sum(buf[slot, :, :D_MODEL] * h_row, axis=-1, keepdims=True)
        gate = jnp.sum(jnp.where(tok_lane == t, gate_ref[...], 0.0), axis=-1, keepdims=True)
        w = gate * jax.nn.gelu(act)
        out = jnp.sum(buf[slot, :, D_MODEL:] * w, axis=0, keepdims=True)
        o_ref[pl.ds(t, 1), :] = x_ref[pl.ds(t, 1), :] + out
        return carry

    lax.fori_loop(0, tt, body, 0)


def _peer_experts(e, gate, x, norm2, uv):
    m = x.shape[0]
    tt = min(128, m)
    n_sel = PEER_HEADS * PEER_TOPK
    return pl.pallas_call(
        _peer_expert_kernel,
        grid=(m // tt,),
        in_specs=[pl.BlockSpec((n_sel, tt), lambda i: (0, i), memory_space=pltpu.SMEM),
                  pl.BlockSpec((n_sel, tt), lambda i: (0, i)),
                  pl.BlockSpec((tt, D_MODEL), lambda i: (i, 0)),
                  pl.BlockSpec((1, D_MODEL), lambda i: (0, 0)),
                  pl.BlockSpec(memory_space=pl.ANY)],
        out_specs=pl.BlockSpec((tt, D_MODEL), lambda i: (i, 0)),
        out_shape=jax.ShapeDtypeStruct((m, D_MODEL), F32),
        scratch_shapes=[pltpu.VMEM((PEER_SLOTS, n_sel, 2 * D_MODEL), F32),
                        pltpu.SemaphoreType.DMA((PEER_SLOTS,)),
                        pltpu.VMEM((tt, D_MODEL), F32)],
        compiler_params=_cp("arbitrary"),
    )(e, gate, x, norm2.reshape(1, -1), uv)


def _pad_rw(z):
    o1 = 3 * RW_WIDTH
    zeros = jnp.zeros(z.shape[:-1] + (LORA_PAD - DECAY_LORA,), z.dtype)
    return jnp.concatenate([z[..., :o1 + DECAY_LORA], zeros,
                            z[..., o1 + DECAY_LORA:o1 + DECAY_LORA + A_LORA], zeros,
                            z[..., o1 + DECAY_LORA + A_LORA:]], axis=-1)


def _unpad_rw(z):
    o1 = 3 * RW_WIDTH
    return jnp.concatenate([z[..., :o1 + DECAY_LORA], z[..., o1 + LORA_PAD:o1 + LORA_PAD + A_LORA],
                            z[..., o1 + 2 * LORA_PAD:]], axis=-1)


def _pad_rows(z, n):
    return jnp.pad(z, ((0, n - z.shape[0]), (0, 0)))


def kernel(x_prompt, x_sample, cache_k_cmp, cache_v_cmp, cache_k_slc, cache_v_slc, state_k_win, state_v_win, state_wkv, state_shift, page_table, norm1, w_in, q_norm, k_norm_slc, k_norm_win, k_norm_cmp, cmp_w1_k, cmp_w2_k, cmp_pe_k, cmp_w1_v, cmp_w2_v, cmp_pe_v, attn_out_norm, rw_mu, rw_w0, rw_w2, rw_a0, rw_a2, rw_g2, rw_k_k, rw_k_a, rw_r_k, rw_ln_w, rw_ln_b, w_out, norm2, peer_wq, peer_q_norm, peer_subkeys, peer_u, peer_v):
    assert x_prompt.shape[-1] == D_MODEL and cache_k_cmp.shape[0] == 1
    b, t, _ = x_prompt.shape
    db, ts, _ = x_sample.shape
    n_pages = page_table.shape[1]
    n_past = n_pages * PAGE
    n_pool = cache_k_cmp.shape[1]
    wbuf = state_k_win.shape[2]
    assert t % Q_BLOCK == 0 and n_past // CMP_STRIDE == (n_past + ts) // CMP_STRIDE and ts <= 8

    w_in0 = w_in[0]
    gates_w = w_in0[:, QKV_WIDTH:QKV_WIDTH + 3 * ATT_HEADS]
    w_qkvg = jnp.concatenate([w_in0[:, :QKV_WIDTH], gates_w,
                              jnp.zeros((D_MODEL, 128 - 3 * ATT_HEADS), F32)], axis=1).astype(BF16)
    w_rw = _pad_rw(w_in0[:, QKV_WIDTH + 3 * ATT_HEADS:]).astype(BF16)
    w_out_bf, wq_bf = w_out[0].astype(BF16), peer_wq[0].astype(BF16)
    uv = jnp.concatenate([peer_u[0], peer_v[0]], axis=1)
    mu_p = _pad_rw(rw_mu[0])
    w2_p, a2_p = _pad_rows(rw_w2[0], LORA_PAD), _pad_rows(rw_a2[0], LORA_PAD)
    pools = (cache_k_cmp, cache_v_cmp, cache_k_slc, cache_v_slc)

    def group(x, pos, seq_len, prev_feat, wkv0):
        nb, tl, _ = x.shape
        x2 = x.reshape(nb * tl, D_MODEL)
        proj = _norm_matmul(x2, norm1[0], w_qkvg, 896)
        feat = _norm_matmul(x2, norm1[0], w_rw, 896)
        q, kc, vc, ks, vs, kw, vw = _qkv_post(proj, pos, q_norm[0], k_norm_slc[0], k_norm_win[0])
        gates = proj[:, QKV_WIDTH:QKV_WIDTH + 3 * ATT_HEADS]
        r, d, k2, v, kk, kka, g, bonus = _rwkv_pre(feat, prev_feat, seq_len, mu_p, rw_w0[0], w2_p, rw_a0[0],
                                                   a2_p, rw_g2[0], rw_k_k[0], rw_k_a[0], rw_r_k[0])
        sq = lambda z: z.reshape(nb, tl, RW_WIDTH)
        o_wkv, s_fin = _wkv_scan(sq(r), sq(d), sq(k2), sq(v), sq(kk), sq(kka), _pack_state(wkv0))
        shift = _unpad_rw(feat.reshape(nb, tl, RW_PROJ_PAD)[:, -1])
        return x2, (q, gates, kc, vc, ks, vs, kw, vw), (o_wkv.reshape(nb * tl, RW_WIDTH), bonus, g), \
            _unpack_state(s_fin), shift

    def finish(x2, o_att, rw):
        x1 = _combine(o_att, *rw, rw_ln_w[0], rw_ln_b[0], attn_out_norm[0], x2, w_out_bf)
        pq = _norm_matmul(x1, norm2[0], wq_bf, 512)
        e, gate = _peer_route(pq, peer_q_norm[0], peer_subkeys[0])
        return _peer_experts(e, gate, x1, norm2[0], uv)

    def compress_kv(pool_k, pool_v, pt):
        ck = _compress(pool_k, pt, cmp_w1_k[0], cmp_w2_k[0], cmp_pe_k[0], k_norm_cmp[0], True)
        cv = _compress(pool_v, pt, cmp_w1_v[0], cmp_w2_v[0], cmp_pe_v[0], k_norm_cmp[0], False)
        return ck, cv

    kv5 = lambda z, nb, tl: z.reshape(1, nb, tl, KV_HEADS, ATT_HD)

    tm_pre = min(256, b * t)
    x2, (q, gates, kc, vc, ks, vs, kw, vw), rw, wkv_p, shift_p = group(
        x_prompt, jnp.arange(t), t, jnp.zeros((tm_pre, RW_PROJ_PAD), F32),
        jnp.zeros((b, RW_HEADS, RW_HD, RW_HD), F32))
    ident = jnp.arange(b * (t // PAGE), dtype=I32).reshape(b, t // PAGE)
    as_pool = lambda z: z.reshape(1, -1, PAGE, KV_HEADS, ATT_HD)
    ck, cv = compress_kv(as_pool(kc), as_pool(vc), ident)
    gates_h = gates.reshape(b * t, KV_HEADS, 3 * ATT_GROUP).transpose(1, 0, 2)
    o_att = _nsa_prompt(q, gates_h, ck, cv, ks, vs, kw, vw, b, t)
    y_prompt = finish(x2, o_att, rw).reshape(b, t, D_MODEL)
    wp = min(WINDOW, t)
    p_state = (kv5(kc, b, t), kv5(vc, b, t), kv5(ks, b, t), kv5(vs, b, t),
               kv5(kw, b, t)[:, :, t - wp:], kv5(vw, b, t)[:, :, t - wp:], wkv_p[None], shift_p[None])

    prev = jnp.repeat(_pad_rw(state_shift[0]), ts, axis=0)
    x2, (q, gates, kc, vc, ks, vs, kw, vw), rw, wkv_s, shift_s = group(
        x_sample, jnp.tile(n_past + jnp.arange(ts), db), ts, prev, state_wkv[0])
    ck, cv = compress_kv(pools[0], pools[1], page_table)
    tpad = 8
    pad_t = lambda z, n: jnp.pad(z.reshape(db, ts, -1), ((0, 0), (0, n - ts), (0, 0)))
    kwb, vwb = state_k_win[0].reshape(db, wbuf, KV_WIDTH), state_v_win[0].reshape(db, wbuf, KV_WIDTH)
    o_att = _nsa_sample(pad_t(q, tpad), pad_t(gates, tpad), ck, cv, pools[2], pools[3], page_table,
                        pad_t(ks, PAGE), pad_t(vs, PAGE), kwb, vwb, pad_t(kw, PAGE), pad_t(vw, PAGE), ts)
    y_sample = finish(x2, o_att[:, :ts].reshape(db * ts, ATT_WIDTH), rw).reshape(db, ts, D_MODEL)
    kw_new = jnp.concatenate([kwb, kw.reshape(db, ts, KV_WIDTH)], axis=1)[:, -wbuf:]
    vw_new = jnp.concatenate([vwb, vw.reshape(db, ts, KV_WIDTH)], axis=1)[:, -wbuf:]
    s_state = (kv5(kc, db, ts), kv5(vc, db, ts), kv5(ks, db, ts), kv5(vs, db, ts),
               kv5(kw_new, db, wbuf), kv5(vw_new, db, wbuf), wkv_s[None], shift_s[None])
    return (y_prompt, y_sample, *p_state, *s_state)
```

```python
import functools
import math

import jax
import jax.numpy as jnp
from jax import lax
from jax.experimental import pallas as pl
from jax.experimental.pallas import tpu as pltpu

F32, BF16, I32 = jnp.float32, jnp.bfloat16, jnp.int32

D_MODEL = 2048
PAGE = 128
ATT_HEADS, ATT_HD, KV_HEADS, ATT_GROUP = 8, 128, 2, 4
ATT_WIDTH, KV_WIDTH = ATT_HEADS * ATT_HD, KV_HEADS * ATT_HD
RW_HD, RW_WIDTH, RW_HEADS = 64, 1024, 16
ROPE_DIM, ROPE_THETA = 32, 500000.0
CMP_LEN, CMP_STRIDE = 32, 16
SLC_BLOCK, SLC_TOPK, WINDOW, Q_BLOCK = 64, 16, 512, 128
DECAY_LORA, A_LORA, GATE_LORA = 96, 96, 256
LORA_PAD = 128
RW_PROJ = 3 * RW_WIDTH + DECAY_LORA + A_LORA + GATE_LORA
RW_PROJ_PAD = 3 * RW_WIDTH + 2 * LORA_PAD + GATE_LORA
QKV_WIDTH = ATT_WIDTH + 6 * KV_WIDTH
QKVG_PAD = QKV_WIDTH + 128
PEER_HEADS, PEER_DK, PEER_TOPK = 8, 256, 16
PEER_SLOTS = 8
PEER_AHEAD = PEER_SLOTS - 1
NORM_EPS, GN_EPS, NEG_INF, FORCE = 1e-6, 64e-5, -1e30, 1e4
REMOVED = -3e38
VMEM_LIMIT = 56 * 1024 * 1024


def _cp(*sem):
    return pltpu.CompilerParams(dimension_semantics=sem, vmem_limit_bytes=VMEM_LIMIT)


def _dot(a, b):
    return jnp.dot(a.astype(BF16), b.astype(BF16), preferred_element_type=F32)


def _dot_nt(a, b):
    return lax.dot_general(a.astype(BF16), b.astype(BF16), (((1,), (1,)), ((), ())),
                           preferred_element_type=F32)


def _split_dot(x, m_bf16, terms):
    out = None
    for _ in range(terms):
        hi = x.astype(BF16)
        part = jnp.dot(hi, m_bf16, preferred_element_type=F32)
        out = part if out is None else out + part
        x = x - hi.astype(F32)
    return out


def _rms(x, g):
    return x * lax.rsqrt(jnp.mean(x * x, axis=-1, keepdims=True) + NORM_EPS) * g


def _masked_softmax(s, mask):
    s = jnp.where(mask, s, NEG_INF)
    e = jnp.where(mask, jnp.exp(s - jnp.max(s, axis=-1, keepdims=True)), 0.0)
    return e / jnp.maximum(jnp.sum(e, axis=-1, keepdims=True), 1e-30)


def _iota(shape, dim):
    return lax.broadcasted_iota(I32, shape, dim)


def _norm_mm_kernel(x_ref, g_ref, w_ref, o_ref, xn_ref):
    @pl.when(pl.program_id(1) == 0)
    def _():
        xn_ref[...] = _rms(x_ref[...], g_ref[...]).astype(BF16)

    o_ref[...] = jnp.dot(xn_ref[...], w_ref[...], preferred_element_type=F32)


def _norm_matmul(x, g, w_bf16, tn):
    m, k = x.shape
    n = w_bf16.shape[1]
    tm = min(512, m)
    return pl.pallas_call(
        _norm_mm_kernel,
        grid=(m // tm, n // tn),
        in_specs=[pl.BlockSpec((tm, k), lambda i, j: (i, 0)),
                  pl.BlockSpec((1, k), lambda i, j: (0, 0)),
                  pl.BlockSpec((k, tn), lambda i, j: (0, j))],
        out_specs=pl.BlockSpec((tm, tn), lambda i, j: (i, j)),
        out_shape=jax.ShapeDtypeStruct((m, n), F32),
        scratch_shapes=[pltpu.VMEM((tm, k), BF16)],
        compiler_params=_cp("arbitrary", "arbitrary"),
    )(x, g.reshape(1, k), w_bf16)


def _qkv_post_kernel(p_ref, cos_ref, sa_ref, sb_ref, gq_ref, gs_ref, gw_ref,
                     q_ref, kc_ref, vc_ref, ks_ref, vs_ref, kw_ref, vw_ref):
    cos, sa, sb = cos_ref[...], sa_ref[...], sb_ref[...]

    def rope(x):
        return x * cos + pltpu.roll(x, ATT_HD - ROPE_DIM // 2, 1) * sa + pltpu.roll(x, ROPE_DIM // 2, 1) * sb

    def head(c):
        return p_ref[:, c * ATT_HD:(c + 1) * ATT_HD]

    for h in range(ATT_HEADS):
        q_ref[:, h * ATT_HD:(h + 1) * ATT_HD] = rope(_rms(head(h), gq_ref[...]))
    for h in range(KV_HEADS):
        sl = slice(h * ATT_HD, (h + 1) * ATT_HD)
        kc_ref[:, sl] = rope(head(8 + h))
        vc_ref[:, sl] = head(10 + h)
        ks_ref[:, sl] = rope(_rms(head(12 + h), gs_ref[...]))
        vs_ref[:, sl] = head(14 + h)
        kw_ref[:, sl] = rope(_rms(head(16 + h), gw_ref[...]))
        vw_ref[:, sl] = head(18 + h)


def _rope_tables(pos):
    half = ROPE_DIM // 2
    inv = ROPE_THETA ** (-jnp.arange(0, ROPE_DIM, 2, dtype=F32) / ROPE_DIM)
    ang = pos.astype(F32)[:, None] * inv[None, :]
    cos, sin = jnp.cos(ang), jnp.sin(ang)
    n = pos.shape[0]
    rest = ATT_HD - ROPE_DIM
    cos_t = jnp.concatenate([cos, cos, jnp.ones((n, rest), F32)], axis=1)
    sin_a = jnp.concatenate([-sin, jnp.zeros((n, half + rest), F32)], axis=1)
    sin_b = jnp.concatenate([jnp.zeros((n, half), F32), sin, jnp.zeros((n, rest), F32)], axis=1)
    return cos_t, sin_a, sin_b


def _qkv_post(proj, pos, q_norm, k_norm_slc, k_norm_win):
    m = proj.shape[0]
    p = pos.shape[0]
    tm = min(256, m, p)
    reps = p // tm
    tabs = _rope_tables(pos)
    row = lambda w: pl.BlockSpec((tm, w), lambda i: (i, 0))
    tab = pl.BlockSpec((tm, ATT_HD), lambda i: (i % reps, 0))
    gain = pl.BlockSpec((1, ATT_HD), lambda i: (0, 0))
    kv = jax.ShapeDtypeStruct((m, KV_WIDTH), F32)
    return pl.pallas_call(
        _qkv_post_kernel,
        grid=(m // tm,),
        in_specs=[row(QKV_WIDTH), tab, tab, tab, gain, gain, gain],
        out_specs=[row(ATT_WIDTH)] + [row(KV_WIDTH)] * 6,
        out_shape=[jax.ShapeDtypeStruct((m, ATT_WIDTH), F32)] + [kv] * 6,
        compiler_params=_cp("arbitrary"),
    )(proj, *tabs, q_norm.reshape(1, -1), k_norm_slc.reshape(1, -1), k_norm_win.reshape(1, -1))


def _cmp_proj_kernel(pt_ref, *refs, n_in):
    x_refs, w_ref, o_ref, xc_ref = refs[:n_in], refs[n_in], refs[n_in + 1], refs[n_in + 2]
    cpp = PAGE // CMP_STRIDE
    half = n_in * cpp
    for g in range(n_in):
        for h in range(KV_HEADS):
            for c in range(CMP_STRIDE):
                xc_ref[h * half + g * cpp:h * half + (g + 1) * cpp, c * ATT_HD:(c + 1) * ATT_HD] = (
                    x_refs[g][pl.ds(c, cpp, stride=CMP_STRIDE), h, :])
    pp = _dot(xc_ref[...], w_ref[...])
    o_ref[0] = pp[:half]
    o_ref[1] = pp[half:]


def _cmp_project(pool, page_table, w1):
    nb, n_pages = page_table.shape
    n_in = math.gcd(32, n_pages)
    cpp = PAGE // CMP_STRIDE
    w1r = w1.reshape(2, CMP_STRIDE * ATT_HD, ATT_HD)
    w1cat = jnp.concatenate([w1r[0], w1r[1]], axis=1).astype(BF16)

    def xspec(g):
        return pl.BlockSpec((None, None, PAGE, KV_HEADS, ATT_HD),
                            lambda b, j, pt: (0, pt[b * n_pages + j * n_in + g], 0, 0, 0))

    return pl.pallas_call(
        functools.partial(_cmp_proj_kernel, n_in=n_in),
        grid_spec=pltpu.PrefetchScalarGridSpec(
            num_scalar_prefetch=1,
            grid=(nb, n_pages // n_in),
            in_specs=[xspec(g) for g in range(n_in)]
            + [pl.BlockSpec((CMP_STRIDE * ATT_HD, 2 * ATT_HD), lambda b, j, pt: (0, 0))],
            out_specs=pl.BlockSpec((None, KV_HEADS, n_in * cpp, 2 * ATT_HD), lambda b, j, pt: (b, 0, j, 0)),
            scratch_shapes=[pltpu.VMEM((KV_HEADS * n_in * cpp, CMP_STRIDE * ATT_HD), F32)]),
        out_shape=jax.ShapeDtypeStruct((nb, KV_HEADS, n_pages * cpp, 2 * ATT_HD), F32),
        compiler_params=_cp("arbitrary", "arbitrary"),
    )(page_table.reshape(-1), *([pool] * n_in), w1cat)


def _cmp_mlp_kernel(pp_ref, pe_ref, w1_ref, w2_ref, gn_ref, o_ref, *, use_norm):
    pp = pp_ref[...]
    n = pp.shape[0]
    bias = _dot(pe_ref[...], w1_ref[...])[0:1]
    hid = pp[:, :ATT_HD] + pltpu.roll(pp[:, ATT_HD:], n - 1, 0) + bias
    out = _dot(jax.nn.gelu(hid), w2_ref[...])
    if use_norm:
        out = _rms(out, gn_ref[...])
    o_ref[...] = jnp.where(_iota((n, 1), 0) < n - 1, out, 0.0)


def _cmp_mlp(pp, pe, w1, w2, gn, use_norm):
    nb, _, n_chunk, _ = pp.shape
    flat = CMP_LEN * ATT_HD
    pe8 = jnp.broadcast_to(pe.reshape(1, flat), (8, flat))
    return pl.pallas_call(
        functools.partial(_cmp_mlp_kernel, use_norm=use_norm),
        grid=(nb, KV_HEADS),
        in_specs=[pl.BlockSpec((None, None, n_chunk, 2 * ATT_HD), lambda b, h: (b, h, 0, 0)),
                  pl.BlockSpec((8, flat), lambda b, h: (0, 0)),
                  pl.BlockSpec((flat, ATT_HD), lambda b, h: (0, 0)),
                  pl.BlockSpec((ATT_HD, ATT_HD), lambda b, h: (0, 0)),
                  pl.BlockSpec((1, ATT_HD), lambda b, h: (0, 0))],
        out_specs=pl.BlockSpec((None, None, n_chunk, ATT_HD), lambda b, h: (b, h, 0, 0)),
        out_shape=jax.ShapeDtypeStruct((nb, KV_HEADS, n_chunk, ATT_HD), F32),
        compiler_params=_cp("arbitrary", "arbitrary"),
    )(pp, pe8, w1.reshape(flat, ATT_HD).astype(BF16), w2.astype(BF16), gn.reshape(1, -1))


def _compress(pool, page_table, w1, w2, pe, gn, use_norm):
    return _cmp_mlp(_cmp_project(pool, page_table, w1), pe, w1, w2, gn, use_norm)


def _overlap(n_rows, n_cols, n_cmp):
    ci, sj = _iota((n_rows, n_cols), 0), _iota((n_rows, n_cols), 1)
    ov = ((ci * CMP_STRIDE < (sj + 1) * SLC_BLOCK) & (ci * CMP_STRIDE + CMP_LEN > sj * SLC_BLOCK)
          & (ci < n_cmp))
    return ov.astype(BF16)


def _select_blocks(imp, qpos, n_slc, top_n):
    blk = _iota(imp.shape, 1)
    cur = qpos // SLC_BLOCK
    forced = (blk == 0) | (blk == cur) | (blk == cur - 1)
    val = jnp.where(forced, FORCE, jnp.where(blk <= cur, imp, -FORCE))
    val = jnp.where(blk < n_slc, val, REMOVED)
    member = jnp.zeros(imp.shape, F32)
    for _ in range(top_n):
        m = jnp.max(val, axis=-1, keepdims=True)
        idx = jnp.min(jnp.where(val == m, blk, imp.shape[1]), axis=-1, keepdims=True)
        hit = blk == idx
        member = jnp.where(hit, 1.0, member)
        val = jnp.where(hit, REMOVED, val)
    return member


def _expand_blocks(member_bf16, first_key, n_keys):
    lanes = member_bf16.shape[1]
    jj, cc = _iota((lanes, n_keys), 0), _iota((lanes, n_keys), 1)
    sel = (jj == (first_key + cc) // SLC_BLOCK).astype(BF16)
    return jnp.dot(member_bf16, sel, preferred_element_type=F32)


def _online_update(carry, s, valid, v_bf16):
    m, l, acc = carry
    s = jnp.where(valid, s, NEG_INF)
    m_new = jnp.maximum(m, jnp.max(s, axis=-1, keepdims=True))
    p = jnp.where(valid, jnp.exp(s - m_new), 0.0)
    alpha = jnp.exp(m - m_new)
    l = l * alpha + jnp.sum(p, axis=-1, keepdims=True)
    acc = acc * alpha + jnp.dot(p.astype(BF16), v_bf16, preferred_element_type=F32)
    return m_new, l, acc


def _nsa_prompt_kernel(q_ref, gt_ref, ck_ref, cv_ref, ks_ref, vs_ref, kw_ref, vw_ref, o_ref,
                       *, n_cmp, n_slc, lanes, top_n, kt, band):
    i = pl.program_id(2)
    s0 = i * Q_BLOCK
    scale = ATT_HD ** -0.5
    rows = ATT_GROUP * Q_BLOCK
    qs = jnp.concatenate([q_ref[:, g * ATT_HD:(g + 1) * ATT_HD] for g in range(ATT_GROUP)],
                         axis=0).astype(BF16)
    qpos4 = s0 + (_iota((rows, 1), 0) & (Q_BLOCK - 1))
    qpos = s0 + _iota((Q_BLOCK, 1), 0)

    nc = ck_ref.shape[0]
    sc = _dot_nt(qs, ck_ref[...]) * scale
    cidx = _iota((1, nc), 1)
    p_cmp = _masked_softmax(sc, (cidx * CMP_STRIDE + CMP_LEN - 1 <= qpos4) & (cidx < n_cmp))
    o_cmp = _dot(p_cmp, cv_ref[...])
    p_sum = sum(p_cmp[g * Q_BLOCK:(g + 1) * Q_BLOCK] for g in range(ATT_GROUP))
    imp = _split_dot(p_sum, _overlap(nc, lanes, n_cmp), 2)
    member = _select_blocks(imp, qpos, n_slc, top_n).astype(BF16)
    member4 = jnp.concatenate([member] * ATT_GROUP, axis=0)

    def body(t, carry):
        k0 = pl.multiple_of(t * kt, kt)
        s = _dot_nt(qs, ks_ref[pl.ds(k0, kt), :]) * scale
        kpos = k0 + _iota((1, kt), 1)
        valid = (_expand_blocks(member4, k0, kt) > 0.5) & (kpos <= qpos4)
        return _online_update(carry, s, valid, vs_ref[pl.ds(k0, kt), :].astype(BF16))

    init = (jnp.full((rows, 1), NEG_INF, F32), jnp.zeros((rows, 1), F32), jnp.zeros((rows, ATT_HD), F32))
    _, l, acc = lax.fori_loop(0, (s0 + Q_BLOCK + kt - 1) // kt, body, init)
    o_slc = acc / jnp.maximum(l, 1e-30)

    start = pl.multiple_of(jnp.maximum(s0 + Q_BLOCK - band, 0), Q_BLOCK)
    wpos = start + _iota((1, band), 1)
    sw = _dot_nt(qs, kw_ref[pl.ds(start, band), :]) * scale
    p_win = _masked_softmax(sw, (wpos <= qpos4) & (wpos > qpos4 - WINDOW))
    o_win = _dot(p_win, vw_ref[pl.ds(start, band), :])

    gate = jax.nn.sigmoid(gt_ref[...])
    for g in range(ATT_GROUP):
        r = slice(g * Q_BLOCK, (g + 1) * Q_BLOCK)
        o_ref[:, g * ATT_HD:(g + 1) * ATT_HD] = (gate[:, 3 * g:3 * g + 1] * o_cmp[r]
                                                 + gate[:, 3 * g + 1:3 * g + 2] * o_slc[r]
                                                 + gate[:, 3 * g + 2:3 * g + 3] * o_win[r])


def _nsa_prompt(q, gates, ck, cv, ks, vs, kw, vw, b, t):
    nq = t // Q_BLOCK
    n_chunk = ck.shape[2]
    n_slc = -(-t // SLC_BLOCK)
    lanes = -(-n_slc // 128) * 128
    kt = 256 if t % 256 == 0 else Q_BLOCK
    band = min(WINDOW + Q_BLOCK, t)
    qspec = pl.BlockSpec((Q_BLOCK, ATT_GROUP * ATT_HD), lambda bb, h, i: (bb * nq + i, h))
    cspec = pl.BlockSpec((None, None, n_chunk, ATT_HD), lambda bb, h, i: (bb, h, 0, 0))
    kspec = pl.BlockSpec((t, ATT_HD), lambda bb, h, i: (bb, h))
    return pl.pallas_call(
        functools.partial(_nsa_prompt_kernel, n_cmp=n_chunk - 1, n_slc=n_slc, lanes=lanes,
                          top_n=min(SLC_TOPK, n_slc), kt=kt, band=band),
        grid=(b, KV_HEADS, nq),
        in_specs=[qspec, pl.BlockSpec((None, Q_BLOCK, 3 * ATT_GROUP), lambda bb, h, i: (h, bb * nq + i, 0)),
                  cspec, cspec, kspec, kspec, kspec, kspec],
        out_specs=qspec,
        out_shape=jax.ShapeDtypeStruct((b * t, ATT_WIDTH), F32),
        compiler_params=_cp("arbitrary", "arbitrary", "arbitrary"),
    )(q, gates, ck, cv, ks, vs, kw, vw)


def _nsa_sample_kernel(pt_ref, q_ref, gt_ref, ck_ref, cv_ref, *refs,
                       n_in, ts, tpad, n_past, n_cmp, n_slc, top_n, wbuf):
    kpages, vpages = refs[:n_in], refs[n_in:2 * n_in]
    (ksn_ref, vsn_ref, kwb_ref, vwb_ref, kwn_ref, vwn_ref, o_ref,
     mem_ref, ocmp_ref, m_ref, l_ref, acc_ref) = refs[2 * n_in:]
    j = pl.program_id(1)
    scale = ATT_HD ** -0.5
    rows = ATT_GROUP * tpad
    tok = _iota((rows, 1), 0) & (tpad - 1)
    qpos = n_past + tok
    nkeys = n_in * PAGE

    def q_of(h):
        return jnp.concatenate([q_ref[:, (h * ATT_GROUP + g) * ATT_HD:(h * ATT_GROUP + g + 1) * ATT_HD]
                                for g in range(ATT_GROUP)], axis=0).astype(BF16)

    def hcols(ref, h):
        return ref[:, h * ATT_HD:(h + 1) * ATT_HD]

    @pl.when(j == 0)
    def _():
        nc = ck_ref.shape[1]
        lanes = mem_ref.shape[2]
        cidx = _iota((1, nc), 1)
        for h in range(KV_HEADS):
            sc = _dot_nt(q_of(h), ck_ref[h]) * scale
            p_cmp = _masked_softmax(sc, (cidx * CMP_STRIDE + CMP_LEN - 1 <= qpos) & (cidx < n_cmp))
            ocmp_ref[h] = _dot(p_cmp, cv_ref[h])
            p_sum = sum(p_cmp[g * tpad:(g + 1) * tpad] for g in range(ATT_GROUP))
            imp = _split_dot(p_sum, _overlap(nc, lanes, n_cmp), 2)
            member = _select_blocks(imp, qpos[:tpad], n_slc, top_n)
            mem_ref[h] = jnp.concatenate([member] * ATT_GROUP, axis=0)
            m_ref[h] = jnp.full((rows, 1), NEG_INF, F32)
            l_ref[h] = jnp.zeros((rows, 1), F32)
            acc_ref[h] = jnp.zeros((rows, ATT_HD), F32)

    k0 = j * nkeys
    kpos = k0 + _iota((1, nkeys), 1)
    for h in range(KV_HEADS):
        kcat = jnp.concatenate([r[:, h, :] for r in kpages], axis=0).astype(BF16)
        vcat = jnp.concatenate([r[:, h, :] for r in vpages], axis=0).astype(BF16)
        s = _dot_nt(q_of(h), kcat) * scale
        valid = (_expand_blocks(mem_ref[h].astype(BF16), k0, nkeys) > 0.5) & (kpos <= qpos)
        m, l, acc = _online_update((m_ref[h], l_ref[h], acc_ref[h]), s, valid, vcat)
        m_ref[h], l_ref[h], acc_ref[h] = m, l, acc

    @pl.when(j == pl.num_programs(1) - 1)
    def _():
        gate = jax.nn.sigmoid(gt_ref[...])
        npad = ksn_ref.shape[0]
        ridx = _iota((1, npad), 1)
        new_ok = (ridx < ts) & (n_past + ridx <= qpos)
        cidx = _iota((1, wbuf + npad), 1)
        wpos = jnp.where(cidx < wbuf, n_past - wbuf + cidx, n_past + cidx - wbuf)
        w_ok = ((cidx - wbuf < ts) & (wpos <= qpos) & (wpos > qpos - WINDOW) & (wpos >= 0))
        for h in range(KV_HEADS):
            qh = q_of(h)
            blk = n_past // SLC_BLOCK
            s = _dot_nt(qh, hcols(ksn_ref, h)) * scale
            valid = (mem_ref[h][:, blk:blk + 1] > 0.5) & new_ok
            _, l, acc = _online_update((m_ref[h], l_ref[h], acc_ref[h]), s, valid,
                                       hcols(vsn_ref, h).astype(BF16))
            o_slc = acc / jnp.maximum(l, 1e-30)
            kw = jnp.concatenate([hcols(kwb_ref, h), hcols(kwn_ref, h)], axis=0)
            vw = jnp.concatenate([hcols(vwb_ref, h), hcols(vwn_ref, h)], axis=0)
            p_win = _masked_softmax(_dot_nt(qh, kw) * scale, w_ok)
            o_win = _dot(p_win, vw)
            o_cmp = ocmp_ref[h]
            for g in range(ATT_GROUP):
                r = slice(g * tpad, (g + 1) * tpad)
                c = 3 * (h * ATT_GROUP + g)
                hh = h * ATT_GROUP + g
                o_ref[:, hh * ATT_HD:(hh + 1) * ATT_HD] = (gate[:, c:c + 1] * o_cmp[r]
                                                           + gate[:, c + 1:c + 2] * o_slc[r]
                                                           + gate[:, c + 2:c + 3] * o_win[r])


def _nsa_sample(q, gates, ck, cv, pool_k, pool_v, page_table, ks_new, vs_new, kw_buf, vw_buf,
                kw_new, vw_new, ts):
    db, tpad, _ = q.shape
    n_pages = page_table.shape[1]
    n_past = n_pages * PAGE
    n_in = math.gcd(8, n_pages)
    n_chunk = ck.shape[2]
    n_slc = -(-(n_past + ts) // SLC_BLOCK)
    lanes = -(-n_slc // 128) * 128
    wbuf = kw_buf.shape[1]
    npad = ks_new.shape[1]
    rows = ATT_GROUP * tpad
    per_b = lambda s1, s2: pl.BlockSpec((None, s1, s2), lambda b, j, pt: (b, 0, 0))

    def pspec(g):
        return pl.BlockSpec((None, None, PAGE, KV_HEADS, ATT_HD),
                            lambda b, j, pt: (0, pt[b * n_pages + j * n_in + g], 0, 0, 0))

    cspec = pl.BlockSpec((None, KV_HEADS, n_chunk, ATT_HD), lambda b, j, pt: (b, 0, 0, 0))
    return pl.pallas_call(
        functools.partial(_nsa_sample_kernel, n_in=n_in, ts=ts, tpad=tpad, n_past=n_past,
                          n_cmp=n_chunk - 1, n_slc=n_slc, top_n=min(SLC_TOPK, n_slc), wbuf=wbuf),
        grid_spec=pltpu.PrefetchScalarGridSpec(
            num_scalar_prefetch=1,
            grid=(db, n_pages // n_in),
            in_specs=[per_b(tpad, ATT_WIDTH), per_b(tpad, 3 * ATT_HEADS), cspec, cspec]
            + [pspec(g) for g in range(n_in)] * 2
            + [per_b(npad, KV_WIDTH)] * 2 + [per_b(wbuf, KV_WIDTH)] * 2 + [per_b(npad, KV_WIDTH)] * 2,
            out_specs=per_b(tpad, ATT_WIDTH),
            scratch_shapes=[pltpu.VMEM((KV_HEADS, rows, lanes), F32),
                            pltpu.VMEM((KV_HEADS, rows, ATT_HD), F32),
                            pltpu.VMEM((KV_HEADS, rows, 1), F32),
                            pltpu.VMEM((KV_HEADS, rows, 1), F32),
                            pltpu.VMEM((KV_HEADS, rows, ATT_HD), F32)]),
        out_shape=jax.ShapeDtypeStruct((db, tpad, ATT_WIDTH), F32),
        compiler_params=_cp("arbitrary", "arbitrary"),
    )(page_table.reshape(-1), q, gates, ck, cv, *([pool_k] * n_in), *([pool_v] * n_in),
      ks_new, vs_new, kw_buf, vw_buf, kw_new, vw_new)


def _seg_mats():
    seg = (_iota((RW_WIDTH, 128), 0) // RW_HD == _iota((RW_WIDTH, 128), 1)).astype(BF16)
    seg_t = (_iota((128, RW_WIDTH), 1) // RW_HD == _iota((128, RW_WIDTH), 0)).astype(BF16)
    return seg, seg_t


def _rwkv_pre_kernel(f_ref, pf_ref, mu_ref, w0_ref, w2_ref, a0_ref, a2_ref, g2_ref, kk_w_ref, ka_w_ref,
                     rk_ref, r_ref, d_ref, k_ref, v_ref, kk_ref, kka_ref, g_ref, bonus_ref, carry_ref,
                     *, seq_len):
    tm = f_ref.shape[0]

    @pl.when(pl.program_id(0) == 0)
    def _():
        carry_ref[...] = jnp.zeros_like(carry_ref)

    feat = f_ref[...]
    row = _iota((tm, 1), 0)
    shifted = jnp.where(row == 0, carry_ref[...], pltpu.roll(feat, 1, 0))
    shifted = jnp.where((pl.program_id(0) * tm + row) % seq_len == 0, pf_ref[...], shifted)
    carry_ref[...] = feat[tm - 1:tm]
    mixed = feat + (shifted - feat) * mu_ref[...]
    w = RW_WIDTH
    r, k, v = mixed[:, :w], mixed[:, w:2 * w], mixed[:, 2 * w:3 * w]
    wd = mixed[:, 3 * w:3 * w + LORA_PAD]
    ad = mixed[:, 3 * w + LORA_PAD:3 * w + 2 * LORA_PAD]
    gd = mixed[:, 3 * w + 2 * LORA_PAD:]
    wlog = -jax.nn.softplus(-(w0_ref[...] + _dot(jnp.tanh(wd), w2_ref[...]))) - 0.5
    decay = jnp.exp(-jnp.exp(wlog))
    a = jax.nn.sigmoid(a0_ref[...] + _dot(ad, a2_ref[...]))
    g = _dot(jax.nn.sigmoid(gd), g2_ref[...])
    seg, seg_t = _seg_mats()
    kk = k * kk_w_ref[...]
    norm = jnp.maximum(jnp.sqrt(_split_dot(kk * kk, seg, 2)), 1e-12)
    kk = kk * _split_dot(1.0 / norm, seg_t, 2)
    k2 = k * (1.0 + (a - 1.0) * ka_w_ref[...])
    bonus = _split_dot(_split_dot(r * k2 * rk_ref[...], seg, 2), seg_t, 2) * v
    r_ref[...], d_ref[...], k_ref[...], v_ref[...] = r, decay, k2, v
    kk_ref[...], kka_ref[...], g_ref[...], bonus_ref[...] = kk, kk * a, g, bonus


def _rwkv_pre(feat, prev, seq_len, mu, w0, w2, a0, a2, g2, k_k, k_a, r_k):
    m = feat.shape[0]
    tm = min(256, m)
    vec = lambda n: pl.BlockSpec((1, n), lambda i: (0, 0))
    mat = lambda a, b: pl.BlockSpec((a, b), lambda i: (0, 0))
    row = pl.BlockSpec((tm, RW_WIDTH), lambda i: (i, 0))
    prev_blocks = prev.shape[0] // tm
    return pl.pallas_call(
        functools.partial(_rwkv_pre_kernel, seq_len=seq_len),
        grid=(m // tm,),
        in_specs=[pl.BlockSpec((tm, RW_PROJ_PAD), lambda i: (i, 0)),
                  pl.BlockSpec((tm, RW_PROJ_PAD), lambda i: (i % prev_blocks, 0)),
                  vec(RW_PROJ_PAD), vec(RW_WIDTH), mat(LORA_PAD, RW_WIDTH), vec(RW_WIDTH),
                  mat(LORA_PAD, RW_WIDTH), mat(GATE_LORA, RW_WIDTH), vec(RW_WIDTH), vec(RW_WIDTH),
                  vec(RW_WIDTH)],
        out_specs=[row] * 8,
        out_shape=[jax.ShapeDtypeStruct((m, RW_WIDTH), F32)] * 8,
        scratch_shapes=[pltpu.VMEM((1, RW_PROJ_PAD), F32)],
        compiler_params=_cp("arbitrary"),
    )(feat, prev, mu.reshape(1, -1), w0.reshape(1, -1), w2.astype(BF16), a0.reshape(1, -1),
      a2.astype(BF16), g2.astype(BF16), k_k.reshape(1, -1), k_a.reshape(1, -1), r_k.reshape(1, -1))


def _wkv_scan_kernel(r_ref, d_ref, k_ref, v_ref, kk_ref, kka_ref, s0_ref, o_ref, sf_ref, st_ref):
    c = pl.program_id(1)
    bg, steps = r_ref.shape[0], r_ref.shape[1]
    chains = [(b, p) for b in range(bg) for p in range(RW_HEADS // 2)]

    @pl.when(c == 0)
    def _():
        st_ref[...] = s0_ref[...]

    diag = ((_iota((RW_HD, 128), 1) & (RW_HD - 1)) == _iota((RW_HD, 128), 0)).astype(F32)
    same_head = ((_iota((128, 128), 0) // RW_HD) == (_iota((128, 128), 1) // RW_HD)).astype(BF16)
    same_head2 = jnp.concatenate([same_head, same_head], axis=0)

    def split(x):
        hi = x.astype(BF16)
        return hi, x - hi.astype(F32)

    def blk(z, i):
        return z[i * RW_HD:(i + 1) * RW_HD]

    def run(base, n):
        tiles = [[ref[b, pl.ds(base, n), :] for b in range(bg)]
                 for ref in (r_ref, d_ref, k_ref, v_ref, kk_ref, kka_ref)]
        s = [st_ref[b, p] for b, p in chains]
        outs = [[] for _ in chains]
        for j in range(n):
            r_, d_, k_, v_, kk_, kka_ = ([t[b][j:j + 1, p * 128:(p + 1) * 128] for b, p in chains]
                                         for t in tiles)
            lhs = []
            for i in range(len(chains)):
                hi, lo = split(s[i] * kk_[i])
                lhs.append(jnp.concatenate([hi, lo.astype(BF16)], axis=1))
            s_kk = jnp.dot(jnp.concatenate(lhs, axis=0), same_head2, preferred_element_type=F32)
            lhs = []
            for i in range(len(chains)):
                hi, lo = split(v_[i])
                lhs.append(jnp.concatenate(
                    [(jnp.broadcast_to(hi.astype(F32), (RW_HD, 128)) * diag).astype(BF16),
                     (jnp.broadcast_to(lo, (RW_HD, 128)) * diag).astype(BF16)], axis=1))
            v_col = jnp.dot(jnp.concatenate(lhs, axis=0), same_head2, preferred_element_type=F32)
            s = [s[i] * d_[i] - blk(s_kk, i) * kka_[i] + blk(v_col, i) * k_[i] for i in range(len(chains))]
            o_sum = jnp.dot(jnp.concatenate([(s[i] * r_[i]).astype(BF16) for i in range(len(chains))], axis=0),
                            same_head, preferred_element_type=F32)
            for i in range(len(chains)):
                outs[i].append(jnp.sum(blk(o_sum, i) * diag, axis=0, keepdims=True))
        for i, (b, p) in enumerate(chains):
            st_ref[b, p] = s[i]
            o_ref[b, pl.ds(base, n), p * 128:(p + 1) * 128] = jnp.concatenate(outs[i], axis=0)

    if steps % 8 == 0:
        def group(i, carry):
            run(pl.multiple_of(i * 8, 8), 8)
            return carry

        lax.fori_loop(0, steps // 8, group, 0)
    else:
        run(0, steps)

    @pl.when(c == pl.num_programs(1) - 1)
    def _():
        sf_ref[...] = st_ref[...]


def _wkv_scan(r, d, k, v, kk, kka, s0):
    b, t, _ = r.shape
    tc = math.gcd(64, t)
    bg = math.gcd(2, b)
    n_pairs = RW_HEADS // 2
    seq = pl.BlockSpec((bg, tc, RW_WIDTH), lambda bb, c: (bb, c, 0))
    st = pl.BlockSpec((bg, n_pairs, RW_HD, 128), lambda bb, c: (bb, 0, 0, 0))
    return pl.pallas_call(
        _wkv_scan_kernel,
        grid=(b // bg, t // tc),
        in_specs=[seq] * 6 + [st],
        out_specs=[seq, st],
        out_shape=[jax.ShapeDtypeStruct((b, t, RW_WIDTH), F32),
                   jax.ShapeDtypeStruct((b, n_pairs, RW_HD, 128), F32)],
        scratch_shapes=[pltpu.VMEM((bg, n_pairs, RW_HD, 128), F32)],
        compiler_params=_cp("arbitrary", "arbitrary"),
    )(r, d, k, v, kk, kka, s0)


def _pack_state(s):
    b = s.shape[0]
    return s.reshape(b, RW_HEADS // 2, 2, RW_HD, RW_HD).transpose(0, 1, 3, 2, 4).reshape(
        b, RW_HEADS // 2, RW_HD, 2 * RW_HD)


def _unpack_state(s):
    b = s.shape[0]
    return s.reshape(b, RW_HEADS // 2, RW_HD, 2, RW_HD).transpose(0, 1, 3, 2, 4).reshape(
        b, RW_HEADS, RW_HD, RW_HD)


def _combine_kernel(oa_ref, ow_ref, bonus_ref, g_ref, lnw_ref, lnb_ref, an_ref, x_ref, w_ref, o_ref, cat_ref):
    @pl.when(pl.program_id(1) == 0)
    def _():
        for h in range(ATT_HEADS):
            sl = slice(h * ATT_HD, (h + 1) * ATT_HD)
            cat_ref[:, sl] = _rms(oa_ref[:, sl], an_ref[:, sl]).astype(BF16)
        seg, seg_t = _seg_mats()
        o = ow_ref[...]
        mean = _split_dot(_split_dot(o, seg, 2) * (1.0 / RW_HD), seg_t, 2)
        oc = o - mean
        var = _split_dot(oc * oc, seg, 2) * (1.0 / RW_HD)
        y = oc * _split_dot(lax.rsqrt(var + GN_EPS), seg_t, 2) * lnw_ref[...] + lnb_ref[...]
        cat_ref[:, ATT_WIDTH:] = ((y + bonus_ref[...]) * g_ref[...]).astype(BF16)

    o_ref[...] = x_ref[...] + jnp.dot(cat_ref[...], w_ref[...], preferred_element_type=F32)


def _combine(o_att, o_wkv, bonus, g, ln_w, ln_b, attn_norm, x, w_out_bf16):
    m = x.shape[0]
    tm = min(512, m)
    tn = 512
    rowa = pl.BlockSpec((tm, ATT_WIDTH), lambda i, j: (i, 0))
    roww = pl.BlockSpec((tm, RW_WIDTH), lambda i, j: (i, 0))
    vec = lambda n: pl.BlockSpec((1, n), lambda i, j: (0, 0))
    return pl.pallas_call(
        _combine_kernel,
        grid=(m // tm, D_MODEL // tn),
        in_specs=[rowa, roww, roww, roww, vec(RW_WIDTH), vec(RW_WIDTH), vec(ATT_WIDTH),
                  pl.BlockSpec((tm, tn), lambda i, j: (i, j)),
                  pl.BlockSpec((ATT_WIDTH + RW_WIDTH, tn), lambda i, j: (0, j))],
        out_specs=pl.BlockSpec((tm, tn), lambda i, j: (i, j)),
        out_shape=jax.ShapeDtypeStruct((m, D_MODEL), F32),
        scratch_shapes=[pltpu.VMEM((tm, ATT_WIDTH + RW_WIDTH), BF16)],
        compiler_params=_cp("arbitrary", "arbitrary"),
    )(o_att, o_wkv, bonus, g, ln_w.reshape(1, -1), ln_b.reshape(1, -1), attn_norm.reshape(1, -1), x,
      w_out_bf16)


def _peer_route_kernel(pq_ref, gq_ref, sk1_ref, sk2_ref, e_ref, gate_ref, *, n_keys):
    tm = pq_ref.shape[0]
    half = PEER_DK // 2
    k = PEER_TOPK
    row = _iota((128, tm), 0)
    krow = _iota((n_keys, tm), 0)
    crow = _iota((k * k, tm), 0)
    rep_a = (_iota((k * k, 128), 0) // k == _iota((k * k, 128), 1)).astype(BF16)
    rep_b = (_iota((k * k, 128), 0) % k == _iota((k * k, 128), 1)).astype(BF16)
    e_all = jnp.zeros((128, tm), F32)
    gate_all = jnp.zeros((128, tm), F32)

    def spread(rep, x, terms):
        out = None
        for _ in range(terms):
            hi = x.astype(BF16)
            part = jnp.dot(rep, hi, preferred_element_type=F32)
            out = part if out is None else out + part
            x = x - hi.astype(F32)
        return out

    def top_keys(s):
        val = jnp.zeros((128, tm), F32)
        idx = jnp.zeros((128, tm), F32)
        for j in range(k):
            m = jnp.max(s, axis=0, keepdims=True)
            pick = jnp.min(jnp.where(s == m, krow, n_keys), axis=0, keepdims=True)
            val = jnp.where(row == j, m, val)
            idx = jnp.where(row == j, pick.astype(F32), idx)
            s = jnp.where(krow == pick, REMOVED, s)
        return val, idx

    for h in range(PEER_HEADS):
        q = _rms(pq_ref[:, h * PEER_DK:(h + 1) * PEER_DK], gq_ref[...])
        v1, i1 = top_keys(_dot_nt(sk1_ref[...], q[:, :half]))
        v2, i2 = top_keys(_dot_nt(sk2_ref[...], q[:, half:]))
        cand = spread(rep_a, v1, 3) + spread(rep_b, v2, 3)
        ecand = spread(rep_a, i1, 1) * n_keys + spread(rep_b, i2, 1)
        best = jnp.full((128, tm), REMOVED, F32)
        for j in range(k):
            m = jnp.max(cand, axis=0, keepdims=True)
            pick = jnp.min(jnp.where(cand == m, crow, k * k), axis=0, keepdims=True)
            hit = crow == pick
            e_j = jnp.max(jnp.where(hit, ecand, -1.0), axis=0, keepdims=True)
            e_all = jnp.where(row == h * k + j, e_j, e_all)
            best = jnp.where(row == h * k + j, m, best)
            cand = jnp.where(hit, REMOVED, cand)
        mine = (row >= h * k) & (row < (h + 1) * k)
        ex = jnp.where(mine, jnp.exp(best - jnp.max(best, axis=0, keepdims=True)), 0.0)
        gate_all = gate_all + ex / jnp.sum(ex, axis=0, keepdims=True)

    e_ref[...] = e_all.astype(I32)
    gate_ref[...] = gate_all


def _peer_route(pq, q_norm, subkeys):
    m = pq.shape[0]
    tm = min(256, m)
    n_keys = subkeys.shape[1]
    half = PEER_DK // 2
    out = pl.BlockSpec((128, tm), lambda i: (0, i))
    return pl.pallas_call(
        functools.partial(_peer_route_kernel, n_keys=n_keys),
        grid=(m // tm,),
        in_specs=[pl.BlockSpec((tm, PEER_HEADS * PEER_DK), lambda i: (i, 0)),
                  pl.BlockSpec((1, PEER_DK), lambda i: (0, 0)),
                  pl.BlockSpec((n_keys, half), lambda i: (0, 0)),
                  pl.BlockSpec((n_keys, half), lambda i: (0, 0))],
        out_specs=[out, out],
        out_shape=[jax.ShapeDtypeStruct((128, m), I32), jax.ShapeDtypeStruct((128, m), F32)],
        compiler_params=_cp("arbitrary"),
    )(pq, q_norm.reshape(1, -1), subkeys[0], subkeys[1])


def _peer_expert_kernel(e_ref, gate_ref, x_ref, n2_ref, uv_hbm, o_ref, buf, sem, hb_ref):
    tt = x_ref.shape[0]
    n_sel = PEER_HEADS * PEER_TOPK
    x = x_ref[...]
    hb_ref[...] = _rms(x, n2_ref[...])
    tok_lane = _iota((n_sel, tt), 1)

    def row_copy(idx, slot, j):
        return pltpu.make_async_copy(uv_hbm.at[pl.ds(idx, 1)], buf.at[slot, pl.ds(j, 1)], sem.at[slot])

    def issue(t, slot):
        for j in range(n_sel):
            row_copy(e_ref[j, t], slot, j).start(priority=j % 2)

    def wait_all(slot):
        pltpu.make_async_copy(uv_hbm.at[pl.ds(0, n_sel)], buf.at[slot], sem.at[slot]).wait()

    for t0 in range(min(PEER_AHEAD, tt)):
        issue(t0, t0)

    def body(t, carry):
        slot = t % PEER_SLOTS

        @pl.when(t + PEER_AHEAD < tt)
        def _():
            issue(t + PEER_AHEAD, (t + PEER_AHEAD) % PEER_SLOTS)

        wait_all(slot)
        h_row = hb_ref[pl.ds(t, 1), :]
        act = jnp.sum(buf[slot, :, :D_MODEL] * h_row, axis=-1, keepdims=True)
        gate = jnp.sum(jnp.where(tok_lane == t, gate_ref[...], 0.0), axis=-1, keepdims=True)
        w = gate * jax.nn.gelu(act)
        out = jnp.sum(buf[slot, :, D_MODEL:] * w, axis=0, keepdims=True)
        o_ref[pl.ds(t, 1), :] = x_ref[pl.ds(t, 1), :] + out
        return carry

    lax.fori_loop(0, tt, body, 0)


def _peer_experts(e, gate, x, norm2, uv):
    m = x.shape[0]
    tt = min(128, m)
    n_sel = PEER_HEADS * PEER_TOPK
    return pl.pallas_call(
        _peer_expert_kernel,
        grid=(m // tt,),
        in_specs=[pl.BlockSpec((n_sel, tt), lambda i: (0, i), memory_space=pltpu.SMEM),
                  pl.BlockSpec((n_sel, tt), lambda i: (0, i)),
                  pl.BlockSpec((tt, D_MODEL), lambda i: (i, 0)),
                  pl.BlockSpec((1, D_MODEL), lambda i: (0, 0)),
                  pl.BlockSpec(memory_space=pl.ANY)],
        out_specs=pl.BlockSpec((tt, D_MODEL), lambda i: (i, 0)),
        out_shape=jax.ShapeDtypeStruct((m, D_MODEL), F32),
        scratch_shapes=[pltpu.VMEM((PEER_SLOTS, n_sel, 2 * D_MODEL), F32),
                        pltpu.SemaphoreType.DMA((PEER_SLOTS,)),
                        pltpu.VMEM((tt, D_MODEL), F32)],
        compiler_params=_cp("arbitrary"),
    )(e, gate, x, norm2.reshape(1, -1), uv)


def _pad_rw(z):
    o1 = 3 * RW_WIDTH
    zeros = jnp.zeros(z.shape[:-1] + (LORA_PAD - DECAY_LORA,), z.dtype)
    return jnp.concatenate([z[..., :o1 + DECAY_LORA], zeros,
                            z[..., o1 + DECAY_LORA:o1 + DECAY_LORA + A_LORA], zeros,
                            z[..., o1 + DECAY_LORA + A_LORA:]], axis=-1)


def _unpad_rw(z):
    o1 = 3 * RW_WIDTH
    return jnp.concatenate([z[..., :o1 + DECAY_LORA], z[..., o1 + LORA_PAD:o1 + LORA_PAD + A_LORA],
                            z[..., o1 + 2 * LORA_PAD:]], axis=-1)


def _pad_rows(z, n):
    return jnp.pad(z, ((0, n - z.shape[0]), (0, 0)))


def kernel(x_prompt, x_sample, cache_k_cmp, cache_v_cmp, cache_k_slc, cache_v_slc, state_k_win, state_v_win, state_wkv, state_shift, page_table, norm1, w_in, q_norm, k_norm_slc, k_norm_win, k_norm_cmp, cmp_w1_k, cmp_w2_k, cmp_pe_k, cmp_w1_v, cmp_w2_v, cmp_pe_v, attn_out_norm, rw_mu, rw_w0, rw_w2, rw_a0, rw_a2, rw_g2, rw_k_k, rw_k_a, rw_r_k, rw_ln_w, rw_ln_b, w_out, norm2, peer_wq, peer_q_norm, peer_subkeys, peer_u, peer_v):
    assert x_prompt.shape[-1] == D_MODEL and cache_k_cmp.shape[0] == 1
    b, t, _ = x_prompt.shape
    db, ts, _ = x_sample.shape
    n_pages = page_table.shape[1]
    n_past = n_pages * PAGE
    n_pool = cache_k_cmp.shape[1]
    wbuf = state_k_win.shape[2]
    assert t % Q_BLOCK == 0 and n_past // CMP_STRIDE == (n_past + ts) // CMP_STRIDE and ts <= 8

    w_in0 = w_in[0]
    gates_w = w_in0[:, QKV_WIDTH:QKV_WIDTH + 3 * ATT_HEADS]
    w_qkvg = jnp.concatenate([w_in0[:, :QKV_WIDTH], gates_w,
                              jnp.zeros((D_MODEL, 128 - 3 * ATT_HEADS), F32)], axis=1).astype(BF16)
    w_rw = _pad_rw(w_in0[:, QKV_WIDTH + 3 * ATT_HEADS:]).astype(BF16)
    w_out_bf, wq_bf = w_out[0].astype(BF16), peer_wq[0].astype(BF16)
    uv = jnp.concatenate([peer_u[0], peer_v[0]], axis=1)
    mu_p = _pad_rw(rw_mu[0])
    w2_p, a2_p = _pad_rows(rw_w2[0], LORA_PAD), _pad_rows(rw_a2[0], LORA_PAD)
    pools = (cache_k_cmp, cache_v_cmp, cache_k_slc, cache_v_slc)

    def group(x, pos, seq_len, prev_feat, wkv0):
        nb, tl, _ = x.shape
        x2 = x.reshape(nb * tl, D_MODEL)
        proj = _norm_matmul(x2, norm1[0], w_qkvg, 896)
        feat = _norm_matmul(x2, norm1[0], w_rw, 896)
        q, kc, vc, ks, vs, kw, vw = _qkv_post(proj, pos, q_norm[0], k_norm_slc[0], k_norm_win[0])
        gates = proj[:, QKV_WIDTH:QKV_WIDTH + 3 * ATT_HEADS]
        r, d, k2, v, kk, kka, g, bonus = _rwkv_pre(feat, prev_feat, seq_len, mu_p, rw_w0[0], w2_p, rw_a0[0],
                                                   a2_p, rw_g2[0], rw_k_k[0], rw_k_a[0], rw_r_k[0])
        sq = lambda z: z.reshape(nb, tl, RW_WIDTH)
        o_wkv, s_fin = _wkv_scan(sq(r), sq(d), sq(k2), sq(v), sq(kk), sq(kka), _pack_state(wkv0))
        shift = _unpad_rw(feat.reshape(nb, tl, RW_PROJ_PAD)[:, -1])
        return x2, (q, gates, kc, vc, ks, vs, kw, vw), (o_wkv.reshape(nb * tl, RW_WIDTH), bonus, g), \
            _unpack_state(s_fin), shift

    def finish(x2, o_att, rw):
        x1 = _combine(o_att, *rw, rw_ln_w[0], rw_ln_b[0], attn_out_norm[0], x2, w_out_bf)
        pq = _norm_matmul(x1, norm2[0], wq_bf, 512)
        e, gate = _peer_route(pq, peer_q_norm[0], peer_subkeys[0])
        return _peer_experts(e, gate, x1, norm2[0], uv)

    def compress_kv(pool_k, pool_v, pt):
        ck = _compress(pool_k, pt, cmp_w1_k[0], cmp_w2_k[0], cmp_pe_k[0], k_norm_cmp[0], True)
        cv = _compress(pool_v, pt, cmp_w1_v[0], cmp_w2_v[0], cmp_pe_v[0], k_norm_cmp[0], False)
        return ck, cv

    kv5 = lambda z, nb, tl: z.reshape(1, nb, tl, KV_HEADS, ATT_HD)

    tm_pre = min(256, b * t)
    x2, (q, gates, kc, vc, ks, vs, kw, vw), rw, wkv_p, shift_p = group(
        x_prompt, jnp.arange(t), t, jnp.zeros((tm_pre, RW_PROJ_PAD), F32),
        jnp.zeros((b, RW_HEADS, RW_HD, RW_HD), F32))
    ident = jnp.arange(b * (t // PAGE), dtype=I32).reshape(b, t // PAGE)
    as_pool = lambda z: z.reshape(1, -1, PAGE, KV_HEADS, ATT_HD)
    ck, cv = compress_kv(as_pool(kc), as_pool(vc), ident)
    gates_h = gates.reshape(b * t, KV_HEADS, 3 * ATT_GROUP).transpose(1, 0, 2)
    o_att = _nsa_prompt(q, gates_h, ck, cv, ks, vs, kw, vw, b, t)
    y_prompt = finish(x2, o_att, rw).reshape(b, t, D_MODEL)
    wp = min(WINDOW, t)
    p_state = (kv5(kc, b, t), kv5(vc, b, t), kv5(ks, b, t), kv5(vs, b, t),
               kv5(kw, b, t)[:, :, t - wp:], kv5(vw, b, t)[:, :, t - wp:], wkv_p[None], shift_p[None])

    prev = jnp.repeat(_pad_rw(state_shift[0]), ts, axis=0)
    x2, (q, gates, kc, vc, ks, vs, kw, vw), rw, wkv_s, shift_s = group(
        x_sample, jnp.tile(n_past + jnp.arange(ts), db), ts, prev, state_wkv[0])
    ck, cv = compress_kv(pools[0], pools[1], page_table)
    tpad = 8
    pad_t = lambda z, n: jnp.pad(z.reshape(db, ts, -1), ((0, 0), (0, n - ts), (0, 0)))
    kwb, vwb = state_k_win[0].reshape(db, wbuf, KV_WIDTH), state_v_win[0].reshape(db, wbuf, KV_WIDTH)
    o_att = _nsa_sample(pad_t(q, tpad), pad_t(gates, tpad), ck, cv, pools[2], pools[3], page_table,
                        pad_t(ks, PAGE), pad_t(vs, PAGE), kwb, vwb, pad_t(kw, PAGE), pad_t(vw, PAGE), ts)
    y_sample = finish(x2, o_att[:, :ts].reshape(db * ts, ATT_WIDTH), rw).reshape(db, ts, D_MODEL)
    kw_new = jnp.concatenate([kwb, kw.reshape(db, ts, KV_WIDTH)], axis=1)[:, -wbuf:]
    vw_new = jnp.concatenate([vwb, vw.reshape(db, ts, KV_WIDTH)], axis=1)[:, -wbuf:]
    s_state = (kv5(kc, db, ts), kv5(vc, db, ts), kv5(ks, db, ts), kv5(vs, db, ts),
               kv5(kw_new, db, wbuf), kv5(vw_new, db, wbuf), wkv_s[None], shift_s[None])
    return (y_prompt, y_sample, *p_state, *s_state)
```
